```python
import math
import jax, jax.numpy as jnp
from jax import lax
import numpy as np

D_MODEL = 1024
BATCH = 8
SEQ = 4096
DEPTH = 4

EXPAND = 2
MIX_WIDTH = EXPAND * D_MODEL
PLE_DIM = 256
N_A = DEPTH // 2
N_B = DEPTH - N_A
EPS = 1e-6

GLA_HEADS = 4
GLA_KEY_WIDTH = MIX_WIDTH // 2
GLA_DK = GLA_KEY_WIDTH // GLA_HEADS
GLA_DV = MIX_WIDTH // GLA_HEADS
GLA_GATE_RANK = 16
GLA_GATE_NORMALIZER = 16.0
GLA_CHUNK = 64
GLA_IN_COLS = 2 * GLA_KEY_WIDTH + 2 * MIX_WIDTH + GLA_GATE_RANK

DIFF_HEAD_DIM = 128
DIFF_HEADS = MIX_WIDTH // (2 * DIFF_HEAD_DIM)
DIFF_QBLOCK = 128
DIFF_IN_COLS = 2 * MIX_WIDTH
KV_COLS = 2 * MIX_WIDTH

kernel_name = "yoco_gla_diffattn_hybrid"


def _rmsnorm(x, g):
    xf = x.astype(jnp.float32)
    y = xf * lax.rsqrt(jnp.mean(xf * xf, axis=-1, keepdims=True) + EPS)
    return (y * g.astype(jnp.float32)).astype(x.dtype)


def _to_chunks(t):
    b, s, h, d = t.shape
    return t.reshape(b, s // GLA_CHUNK, GLA_CHUNK, h, d).transpose(1, 0, 3, 2, 4)


def _gla_mixer(xn, w_in, w_gk2, b_gk, g_norm, w_out):
    bsz, s, _ = xn.shape
    proj = xn @ w_in
    q, k, v, gate, gk_lr = jnp.split(
        proj, [GLA_KEY_WIDTH, 2 * GLA_KEY_WIDTH, 2 * GLA_KEY_WIDTH + MIX_WIDTH,
               2 * GLA_KEY_WIDTH + 2 * MIX_WIDTH], axis=-1)
    gk = gk_lr @ w_gk2 + b_gk
    log_a = jax.nn.log_sigmoid(gk.astype(jnp.float32)) / GLA_GATE_NORMALIZER
    f32 = jnp.float32
    qc = _to_chunks(q.reshape(bsz, s, GLA_HEADS, GLA_DK).astype(f32) * GLA_DK ** -0.5)
    kc = _to_chunks(k.reshape(bsz, s, GLA_HEADS, GLA_DK).astype(f32))
    vc = _to_chunks(v.reshape(bsz, s, GLA_HEADS, GLA_DV).astype(f32))
    bc = jnp.cumsum(_to_chunks(log_a.reshape(bsz, s, GLA_HEADS, GLA_DK)), axis=3)
    causal = jnp.tril(jnp.ones((GLA_CHUNK, GLA_CHUNK), dtype=bool))

    def step(state, inp):
        qi, ki, vi, bi = inp
        inter = jnp.einsum('bhid,bhde->bhie', qi * jnp.exp(bi), state)
        decay = bi[:, :, :, None, :] - bi[:, :, None, :, :]
        decay = jnp.exp(jnp.where(causal[None, None, :, :, None], decay, -jnp.inf))
        scores = jnp.einsum('bhid,bhijd,bhjd->bhij', qi, decay, ki)
        intra = jnp.einsum('bhij,bhje->bhie', scores, vi)
        b_last = bi[:, :, -1, :]
        k_dec = ki * jnp.exp(b_last[:, :, None, :] - bi)
        new_state = jnp.exp(b_last)[..., None] * state + jnp.einsum('bhjd,bhje->bhde', k_dec, vi)
        return new_state, inter + intra

    s0 = jnp.zeros((bsz, GLA_HEADS, GLA_DK, GLA_DV), f32)
    _, o = lax.scan(step, s0, (qc, kc, vc, bc))
    o = o.transpose(1, 0, 3, 2, 4).reshape(bsz, s, GLA_HEADS, GLA_DV)
    o = _rmsnorm(o, g_norm).reshape(bsz, s, MIX_WIDTH)
    o = (o * jax.nn.silu(gate.astype(f32))).astype(xn.dtype)
    return o @ w_out


def _shared_kv(h, kv_norm, w_kv):
    bsz, s, _ = h.shape
    kv = _rmsnorm(h, kv_norm) @ w_kv
    k, v = jnp.split(kv, [MIX_WIDTH], axis=-1)
    k = k.reshape(bsz, s, DIFF_HEADS, 2, DIFF_HEAD_DIM).transpose(0, 2, 3, 1, 4)
    v = v.reshape(bsz, s, DIFF_HEADS, 2 * DIFF_HEAD_DIM).transpose(0, 2, 1, 3)
    return k[:, :, 0], k[:, :, 1], v


def _diff_mixer(xn, k1, k2, v, w_in, lam, g_norm, w_out, lambda_init):
    bsz, s, _ = xn.shape
    q, gate = jnp.split(xn @ w_in, [MIX_WIDTH], axis=-1)
    q = q.reshape(bsz, s, DIFF_HEADS, 2, DIFF_HEAD_DIM)
    nb = s // DIFF_QBLOCK

    def blocks(t):
        return t.reshape(bsz, nb, DIFF_QBLOCK, DIFF_HEADS, DIFF_HEAD_DIM).transpose(1, 0, 3, 2, 4)

    q1b, q2b = blocks(q[..., 0, :]), blocks(q[..., 1, :])
    lamf = lam.astype(jnp.float32)
    lam_full = (jnp.exp(jnp.sum(lamf[0] * lamf[1])) - jnp.exp(jnp.sum(lamf[2] * lamf[3]))
                + lambda_init)
    scale = DIFF_HEAD_DIM ** -0.5
    kpos = jnp.arange(s, dtype=jnp.int32)
    starts = jnp.arange(nb, dtype=jnp.int32) * DIFF_QBLOCK

    def block(args):
        qb1, qb2, start = args
        mask = kpos[None, :] <= (start + jnp.arange(DIFF_QBLOCK, dtype=jnp.int32))[:, None]
        s1 = jnp.einsum('bhqd,bhkd->bhqk', qb1, k1).astype(jnp.float32) * scale
        s2 = jnp.einsum('bhqd,bhkd->bhqk', qb2, k2).astype(jnp.float32) * scale
        p1 = jax.nn.softmax(jnp.where(mask, s1, -jnp.inf), axis=-1)
        p2 = jax.nn.softmax(jnp.where(mask, s2, -jnp.inf), axis=-1)
        attn = (p1 - lam_full * p2).astype(v.dtype)
        return jnp.einsum('bhqk,bhkd->bhqd', attn, v)

    o = lax.map(block, (q1b, q2b, starts))
    o = o.transpose(1, 0, 3, 2, 4).reshape(bsz, s, DIFF_HEADS, 2 * DIFF_HEAD_DIM)
    o = _rmsnorm(o, g_norm) * (1.0 - lambda_init)
    o = o.reshape(bsz, s, MIX_WIDTH)
    o = (o.astype(jnp.float32) * jax.nn.silu(gate.astype(jnp.float32))).astype(xn.dtype)
    return o @ w_out


def _ple(h, p_i, g_norm, w_gate, w_proj):
    gate = jax.nn.sigmoid((_rmsnorm(h, g_norm) @ w_gate).astype(jnp.float32))
    return (gate * (p_i @ w_proj).astype(jnp.float32)).astype(h.dtype)


def setup_inputs(seed: int = 0) -> dict:
    key = jax.random.key(seed)
    ks = jax.random.split(key, 20)
    f32 = jnp.float32

    def nrm(k, shape, scale):
        return jax.random.normal(k, shape, f32) * scale

    def gain(k, shape):
        return 1.0 + 0.05 * jax.random.normal(k, shape, f32)

    return {
        "x": nrm(ks[0], (BATCH, SEQ, D_MODEL), 1.0),
        "p": nrm(ks[1], (DEPTH, BATCH, SEQ, PLE_DIM), 1.0),
        "norm_mix": gain(ks[2], (DEPTH, D_MODEL)),
        "gla_w_in": nrm(ks[3], (N_A, D_MODEL, GLA_IN_COLS), D_MODEL ** -0.5),
        "gla_w_gk2": nrm(ks[4], (N_A, GLA_GATE_RANK, GLA_KEY_WIDTH), GLA_GATE_RANK ** -0.5),
        "gla_b_gk": nrm(ks[5], (N_A, GLA_KEY_WIDTH), 0.1),
        "gla_norm": gain(ks[6], (N_A, GLA_DV)),
        "gla_w_out": nrm(ks[7], (N_A, MIX_WIDTH, D_MODEL), MIX_WIDTH ** -0.5),
        "kv_norm": gain(ks[8], (D_MODEL,)),
        "w_kv": nrm(ks[9], (D_MODEL, KV_COLS), D_MODEL ** -0.5),
        "diff_w_in": nrm(ks[10], (N_B, D_MODEL, DIFF_IN_COLS), D_MODEL ** -0.5),
        "diff_lambda": nrm(ks[11], (N_B, 4, DIFF_HEAD_DIM), 0.1),
        "diff_norm": gain(ks[12], (N_B, 2 * DIFF_HEAD_DIM)),
        "diff_w_out": nrm(ks[13], (N_B, MIX_WIDTH, D_MODEL), MIX_WIDTH ** -0.5),
        "ple_norm": gain(ks[14], (DEPTH, D_MODEL)),
        "ple_w_gate": nrm(ks[15], (DEPTH, D_MODEL, D_MODEL), D_MODEL ** -0.5),
        "ple_w_proj": nrm(ks[16], (DEPTH, PLE_DIM, D_MODEL), PLE_DIM ** -0.5),
        "final_norm": gain(ks[17], (D_MODEL,)),
    }


def reference(x, p, norm_mix, gla_w_in, gla_w_gk2, gla_b_gk, gla_norm, gla_w_out,
              kv_norm, w_kv, diff_w_in, diff_lambda, diff_norm, diff_w_out,
              ple_norm, ple_w_gate, ple_w_proj, final_norm):
    h = x
    k1 = k2 = v = None
    for i in range(DEPTH):
        hn = _rmsnorm(h, norm_mix[i])
        if i < N_A:
            h = h + _gla_mixer(hn, gla_w_in[i], gla_w_gk2[i], gla_b_gk[i], gla_norm[i], gla_w_out[i])
        else:
            j = i - N_A
            lambda_init = 0.8 - 0.6 * math.exp(-0.3 * i)
            h = h + _diff_mixer(hn, k1, k2, v, diff_w_in[j], diff_lambda[j], diff_norm[j],
                                diff_w_out[j], lambda_init)
        h = h + _ple(h, p[i], ple_norm[i], ple_w_gate[i], ple_w_proj[i])
        if i == N_A - 1:
            k1, k2, v = _shared_kv(h, kv_norm, w_kv)
    return _rmsnorm(h, final_norm)
```

```python
import functools
import math

import jax
import jax.numpy as jnp
from jax import lax
from jax.experimental import pallas as pl
from jax.experimental.pallas import tpu as pltpu

F32 = jnp.float32
BF16 = jnp.bfloat16
EPS = 1e-6
LOG2E = math.log2(math.e)

EXPAND = 2
PLE_DIM = 256
GLA_HEADS = 4
GLA_GATE_RANK = 16
GLA_GATE_NORMALIZER = 16.0
DIFF_HEAD_DIM = 128

LANES = 128
VMEM_LIMIT = 48 * 1024 * 1024

GLA_CHUNK = 64
GLA_SUB = 16
GLA_ROWS = 256
ATTN_TILE = 512
ROW_TILE = 1024
OUT_ROW_TILE = 512
COL_TILE = 1024


def _rms(x, g):
    return x * lax.rsqrt(jnp.mean(x * x, axis=-1, keepdims=True) + EPS) * g


def _split_bf16(x):
    hi = x.astype(BF16)
    lo = (x - hi.astype(F32)).astype(BF16)
    return hi, lo


def _dot(a, b):
    return jnp.dot(a, b, preferred_element_type=F32)


def _dot_nt(a, b):
    return lax.dot_general(a, b, (((1,), (1,)), ((), ())), preferred_element_type=F32)


def _dot_tn(a, b):
    return lax.dot_general(a, b, (((0,), (0,)), ((), ())), preferred_element_type=F32)


def _dot_split(a, b):
    ah, al = _split_bf16(a)
    bh, bl = _split_bf16(b)
    return _dot(ah, bh) + _dot(al, bh) + _dot(ah, bl)


def _norm_matmul_body(*refs, scaled_tiles, scale, lowrank):
    if lowrank:
        x_ref, g_ref, w_ref, wl_ref, o_ref, lr_ref, xn_ref = refs
    else:
        x_ref, g_ref, w_ref, o_ref, xn_ref = refs
    j = pl.program_id(1)

    @pl.when(j == 0)
    def _():
        xn = _rms(x_ref[...], g_ref[...])
        xn_ref[...] = xn.astype(BF16)
        if lowrank:
            lr_ref[...] = _dot_split(xn, wl_ref[...])

    acc = _dot(xn_ref[...], w_ref[...])
    if scaled_tiles:
        acc = acc * jnp.where(j < scaled_tiles, scale, 1.0).astype(F32)
    o_ref[...] = acc.astype(o_ref.dtype)


def _norm_matmul(x, g, w, *, scaled_cols=0, scale=1.0, w_lowrank=None):
    t, d = x.shape
    n = w.shape[1]
    tm = min(ROW_TILE, t)
    tn = min(COL_TILE, n)
    assert t % tm == 0 and n % tn == 0 and scaled_cols % tn == 0
    lowrank = w_lowrank is not None
    in_specs = [
        pl.BlockSpec((tm, d), lambda i, j: (i, 0)),
        pl.BlockSpec((1, d), lambda i, j: (0, 0)),
        pl.BlockSpec((d, tn), lambda i, j: (0, j)),
    ]
    args = [x, g.reshape(1, d), w]
    out_shape = [jax.ShapeDtypeStruct((t, n), BF16)]
    out_specs = [pl.BlockSpec((tm, tn), lambda i, j: (i, j))]
    if lowrank:
        in_specs.append(pl.BlockSpec((d, LANES), lambda i, j: (0, 0)))
        args.append(w_lowrank)
        out_shape.append(jax.ShapeDtypeStruct((t, LANES), F32))
        out_specs.append(pl.BlockSpec((tm, LANES), lambda i, j: (i, 0)))
    out = pl.pallas_call(
        functools.partial(_norm_matmul_body, scaled_tiles=scaled_cols // tn, scale=scale, lowrank=lowrank),
        grid=(t // tm, n // tn),
        in_specs=in_specs,
        out_specs=out_specs,
        out_shape=out_shape,
        scratch_shapes=[pltpu.VMEM((tm, d), BF16)],
        compiler_params=pltpu.CompilerParams(
            dimension_semantics=("parallel", "arbitrary"), vmem_limit_bytes=VMEM_LIMIT),
        name="norm_matmul",
    )(*args)
    return out if lowrank else out[0]


def _log_sigmoid(x):
    return jnp.minimum(x, 0.0) - jnp.log1p(jnp.exp(-jnp.abs(x)))


def _gla_chunk(q, k, v, gate, glr, w2, bias, gnorm, st_ref, a_ref):
    c, dk = q.shape
    nsub = c // GLA_SUB
    qf = q.astype(F32)
    kf = k.astype(F32)

    gk = _dot_split(glr, w2) + bias
    log_a = _log_sigmoid(gk) * (1.0 / GLA_GATE_NORMALIZER)
    row = lax.broadcasted_iota(jnp.int32, (c, c), 0)
    col = lax.broadcasted_iota(jnp.int32, (c, c), 1)
    tri = jnp.where(col <= row, 1.0, 0.0).astype(BF16)
    la_hi, la_lo = _split_bf16(log_a)
    b = _dot(tri, la_hi) + _dot(tri, la_lo)
    b_last = b[c - 1:c, :]

    st = st_ref[...]
    inter = _dot_nt((qf * jnp.exp(b)).astype(BF16), st.astype(BF16))

    rows_s = lax.broadcasted_iota(jnp.int32, (GLA_SUB, c), 0)
    cols_s = lax.broadcasted_iota(jnp.int32, (GLA_SUB, c), 1)
    for i_sub in range(nsub):
        lo = i_sub * GLA_SUB
        b_i = b[lo:lo + GLA_SUB, :]
        q_i = qf[lo:lo + GLA_SUB, :]
        diag = jnp.zeros((GLA_SUB, c), F32)
        for jl in range(GLA_SUB):
            j = lo + jl
            e = jnp.exp(jnp.minimum(b_i - b[j:j + 1, :], 0.0))
            colsum = jnp.sum(q_i * e * kf[j:j + 1, :], axis=1, keepdims=True)
            diag = jnp.where(cols_s == j, colsum, diag)
        if i_sub:
            r = b[lo - 1:lo, :]
            q_t = (q_i * jnp.exp(b_i - r)).astype(BF16)
            k_t = (kf * jnp.exp(jnp.minimum(r - b, 0.0))).astype(BF16)
            off = _dot_nt(q_t, k_t)
            blk = jnp.where(cols_s < lo, off, diag)
        else:
            blk = diag
        a_ref[lo:lo + GLA_SUB, :] = jnp.where(cols_s <= rows_s + lo, blk, 0.0)

    o = inter + _dot(a_ref[...].astype(BF16), v)

    k_dec = (kf * jnp.exp(b_last - b)).astype(BF16)
    st_ref[...] = st * jnp.exp(b_last) + _dot_tn(v, k_dec)

    o = _rms(o, gnorm)
    g = gate.astype(F32)
    return o * (g * jax.nn.sigmoid(g))


def _gla_body(q_ref, k_ref, v_ref, gate_ref, glr_ref, w2_ref, bias_ref, gn_ref, o_ref, st_ref, a_ref,
              *, chunk):
    @pl.when(pl.program_id(2) == 0)
    def _():
        st_ref[...] = jnp.zeros_like(st_ref)

    w2 = w2_ref[...]
    bias = bias_ref[...]
    gnorm = gn_ref[...]

    def step(s, carry):
        r0 = pl.multiple_of(s * chunk, chunk)
        rows = pl.ds(r0, chunk)
        o = _gla_chunk(q_ref[rows, :], k_ref[rows, :], v_ref[rows, :], gate_ref[rows, :], glr_ref[rows, :],
                       w2, bias, gnorm, st_ref, a_ref)
        o_ref[rows, :] = o.astype(o_ref.dtype)
        return carry

    lax.fori_loop(0, q_ref.shape[0] // chunk, step, 0)


def _gla_mixer(proj, glr, w_gk2, b_gk, g_norm, *, batch, seq):
    t = proj.shape[0]
    key_w = w_gk2.shape[1]
    dk = key_w // GLA_HEADS
    mix_w = 2 * key_w
    dv = mix_w // GLA_HEADS
    rows = min(GLA_ROWS, seq)
    nblk = seq // rows
    assert seq % rows == 0 and rows % GLA_CHUNK == 0 and GLA_CHUNK % GLA_SUB == 0
    k_off = key_w // dk
    v_off = 2 * key_w // dv
    g_off = (2 * key_w + mix_w) // dv
    w2 = jnp.zeros((LANES, key_w), F32).at[:GLA_GATE_RANK].set(w_gk2)

    def rowmap(off):
        return lambda b, h, n: (b * nblk + n, off + h)

    return pl.pallas_call(
        functools.partial(_gla_body, chunk=GLA_CHUNK),
        grid=(batch, GLA_HEADS, nblk),
        in_specs=[
            pl.BlockSpec((rows, dk), rowmap(0)),
            pl.BlockSpec((rows, dk), rowmap(k_off)),
            pl.BlockSpec((rows, dv), rowmap(v_off)),
            pl.BlockSpec((rows, dv), rowmap(g_off)),
            pl.BlockSpec((rows, LANES), lambda b, h, n: (b * nblk + n, 0)),
            pl.BlockSpec((LANES, dk), lambda b, h, n: (0, h)),
            pl.BlockSpec((1, dk), lambda b, h, n: (0, h)),
            pl.BlockSpec((1, dv), lambda b, h, n: (0, 0)),
        ],
        out_specs=pl.BlockSpec((rows, dv), rowmap(0)),
        out_shape=jax.ShapeDtypeStruct((t, mix_w), BF16),
        scratch_shapes=[pltpu.VMEM((dv, dk), F32), pltpu.VMEM((GLA_CHUNK, GLA_CHUNK), F32)],
        compiler_params=pltpu.CompilerParams(
            dimension_semantics=("parallel", "parallel", "arbitrary"), vmem_limit_bytes=VMEM_LIMIT),
        name="gla_mixer",
    )(proj, proj, proj, proj, glr, w2, b_gk.reshape(1, key_w), g_norm.reshape(1, dv))


def _diff_attn_body(q_ref, k_ref, v_ref, gate_ref, lam_ref, gn_ref, o_ref, m_ref, l_ref, acc_ref,
                    *, tile, lambda_init):
    qi = pl.program_id(2)
    d = DIFF_HEAD_DIM
    c_exp = (d ** -0.5) * LOG2E
    q = q_ref[...]
    m_ref[...] = jnp.full(m_ref.shape, -1e30, F32)
    l_ref[...] = jnp.zeros_like(l_ref)
    acc_ref[...] = jnp.zeros_like(acc_ref)
    reps = tile // LANES

    def block(j, masked):
        rows = pl.ds(pl.multiple_of(j * tile, tile), tile)
        kb = k_ref[rows, :]
        vb = v_ref[rows, :]
        for t in range(2):
            s = _dot_nt(q[:, t * d:(t + 1) * d], kb[:, t * d:(t + 1) * d])
            if masked:
                row = lax.broadcasted_iota(jnp.int32, s.shape, 0)
                col = lax.broadcasted_iota(jnp.int32, s.shape, 1)
                s = jnp.where(col <= row, s, -1e30)
            m_old = m_ref[t]
            m_new = jnp.maximum(m_old, jnp.max(s, axis=1, keepdims=True))
            alpha = jnp.exp2((m_old - m_new) * c_exp)
            p = jnp.exp2((s - pltpu.repeat(m_new, reps, axis=1)) * c_exp)
            l_ref[t] = alpha * l_ref[t] + jnp.sum(p, axis=1, keepdims=True)
            m_ref[t] = m_new
            acc_ref[t] = acc_ref[t] * pltpu.repeat(alpha, 2 * d // LANES, axis=1) + _dot(p.astype(BF16), vb)

    def full_block(j, carry):
        block(j, False)
        return carry

    lax.fori_loop(0, qi, full_block, 0)
    block(qi, True)

    lam = lam_ref[...]
    lam_full = (jnp.exp(jnp.sum(lam[0:1] * lam[1:2], axis=1, keepdims=True))
                - jnp.exp(jnp.sum(lam[2:3] * lam[3:4], axis=1, keepdims=True)) + lambda_init)
    o1 = acc_ref[0] / pltpu.repeat(l_ref[0], 2 * d // LANES, axis=1)
    o2 = acc_ref[1] / pltpu.repeat(l_ref[1], 2 * d // LANES, axis=1)
    o = _rms(o1 - lam_full * o2, gn_ref[...]) * (1.0 - lambda_init)
    g = gate_ref[...].astype(F32)
    o_ref[...] = (o * (g * jax.nn.sigmoid(g))).astype(o_ref.dtype)


def _diff_mixer(qg, kv, lam, g_norm, lambda_init, *, batch, seq):
    t = qg.shape[0]
    mix_w = qg.shape[1] // 2
    hw = 2 * DIFF_HEAD_DIM
    heads = mix_w // hw
    tile = min(ATTN_TILE, seq)
    nq = seq // tile
    assert seq % tile == 0
    return pl.pallas_call(
        functools.partial(_diff_attn_body, tile=tile, lambda_init=lambda_init),
        grid=(batch, heads, nq),
        in_specs=[
            pl.BlockSpec((tile, hw), lambda b, h, i: (b * nq + i, h)),
            pl.BlockSpec((seq, hw), lambda b, h, i: (b, h)),
            pl.BlockSpec((seq, hw), lambda b, h, i: (b, heads + h)),
            pl.BlockSpec((tile, hw), lambda b, h, i: (b * nq + i, heads + h)),
            pl.BlockSpec((4, DIFF_HEAD_DIM), lambda b, h, i: (0, 0)),
            pl.BlockSpec((1, hw), lambda b, h, i: (0, 0)),
        ],
        out_specs=pl.BlockSpec((tile, hw), lambda b, h, i: (b * nq + i, h)),
        out_shape=jax.ShapeDtypeStruct((t, mix_w), BF16),
        scratch_shapes=[
            pltpu.VMEM((2, tile, LANES), F32),
            pltpu.VMEM((2, tile, LANES), F32),
            pltpu.VMEM((2, tile, hw), F32),
        ],
        compiler_params=pltpu.CompilerParams(
            dimension_semantics=("parallel", "parallel", "arbitrary"), vmem_limit_bytes=VMEM_LIMIT),
        name="diff_attn",
    )(qg, kv, kv, qg, lam, g_norm.reshape(1, hw))


def _out_ple_body(o_ref, h_ref, p_ref, wo_ref, pn_ref, wg_ref, wp_ref, fn_ref, out_ref, *, final):
    hm = h_ref[...] + _dot(o_ref[...], wo_ref[...])
    hn = _rms(hm, pn_ref[...]).astype(BF16)
    gate = jax.nn.sigmoid(_dot(hn, wg_ref[...]))
    emb = _dot(p_ref[...].astype(BF16), wp_ref[...])
    h2 = hm + gate * emb
    if final:
        h2 = _rms(h2, fn_ref[...])
    out_ref[...] = h2


def _out_ple(o, h, p, w_out, ple_norm, w_gate, w_proj, final_norm, *, final):
    t, d = h.shape
    mix_w = o.shape[1]
    pd = p.shape[1]
    tm = min(OUT_ROW_TILE, t)
    assert t % tm == 0

    def const(shape):
        return pl.BlockSpec(shape, lambda i: (0, 0))

    return pl.pallas_call(
        functools.partial(_out_ple_body, final=final),
        grid=(t // tm,),
        in_specs=[
            pl.BlockSpec((tm, mix_w), lambda i: (i, 0)),
            pl.BlockSpec((tm, d), lambda i: (i, 0)),
            pl.BlockSpec((tm, pd), lambda i: (i, 0)),
            const((mix_w, d)),
            const((1, d)),
            const((d, d)),
            const((pd, d)),
            const((1, d)),
        ],
        out_specs=pl.BlockSpec((tm, d), lambda i: (i, 0)),
        out_shape=jax.ShapeDtypeStruct((t, d), F32),
        compiler_params=pltpu.CompilerParams(
            dimension_semantics=("parallel",), vmem_limit_bytes=VMEM_LIMIT),
        name="out_ple",
    )(o, h, p, w_out, ple_norm.reshape(1, d), w_gate, w_proj, final_norm.reshape(1, d))


def kernel(x, p, norm_mix, gla_w_in, gla_w_gk2, gla_b_gk, gla_norm, gla_w_out, kv_norm, w_kv, diff_w_in,
           diff_lambda, diff_norm, diff_w_out, ple_norm, ple_w_gate, ple_w_proj, final_norm):
    batch, seq, d = x.shape
    depth = p.shape[0]
    n_gla = gla_w_in.shape[0]
    t = batch * seq
    key_w = gla_w_gk2.shape[2]
    main_cols = gla_w_in.shape[2] - GLA_GATE_RANK
    mix_w = diff_w_in.shape[2] // 2

    h = x.reshape(t, d)
    kv = None
    for i in range(depth):
        if i < n_gla:
            w_main = gla_w_in[i, :, :main_cols].astype(BF16)
            w_lr = jnp.zeros((d, LANES), F32).at[:, :GLA_GATE_RANK].set(gla_w_in[i, :, main_cols:])
            proj, glr = _norm_matmul(h, norm_mix[i], w_main, scaled_cols=key_w,
                                     scale=(key_w // GLA_HEADS) ** -0.5, w_lowrank=w_lr)
            o = _gla_mixer(proj, glr, gla_w_gk2[i], gla_b_gk[i], gla_norm[i], batch=batch, seq=seq)
            w_out = gla_w_out[i]
        else:
            j = i - n_gla
            lambda_init = 0.8 - 0.6 * math.exp(-0.3 * i)
            qg = _norm_matmul(h, norm_mix[i], diff_w_in[j].astype(BF16))
            o = _diff_mixer(qg, kv, diff_lambda[j], diff_norm[j], lambda_init, batch=batch, seq=seq)
            w_out = diff_w_out[j]
        h = _out_ple(o, h, p[i].reshape(t, p.shape[3]), w_out.astype(BF16), ple_norm[i],
                     ple_w_gate[i].astype(BF16), ple_w_proj[i].astype(BF16), final_norm,
                     final=(i == depth - 1))
        if i == n_gla - 1:
            kv = _norm_matmul(h, kv_norm, w_kv.astype(BF16))
    return h.reshape(batch, seq, d)
```

```python
import functools
import math

import jax
import jax.numpy as jnp
import numpy as np
from jax import lax
from jax.experimental import pallas as pl
from jax.experimental.pallas import tpu as pltpu

F32 = jnp.float32
BF16 = jnp.bfloat16
EPS = 1e-6
LOG2E = math.log2(math.e)

EXPAND = 2
PLE_DIM = 256
GLA_HEADS = 4
GLA_GATE_RANK = 16
GLA_GATE_NORMALIZER = 16.0
DIFF_HEAD_DIM = 128

LANES = 128
VMEM_LIMIT = 56 * 1024 * 1024

GLA_CHUNK = 128
GLA_ROWS = 2048
ATTN_TILE = 512
ROW_TILE = 1024
OUT_ROW_TILE = 512
COL_TILE = 1024


def _rms(x, g):
    return x * lax.rsqrt(jnp.mean(x * x, axis=-1, keepdims=True) + EPS) * g


def _split_bf16(x):
    hi = x.astype(BF16)
    lo = (x - hi.astype(F32)).astype(BF16)
    return hi, lo


def _dot(a, b):
    return jnp.dot(a, b, preferred_element_type=F32)


def _dot_nt(a, b):
    return lax.dot_general(a, b, (((1,), (1,)), ((), ())), preferred_element_type=F32)


def _dot_tn(a, b):
    return lax.dot_general(a, b, (((0,), (0,)), ((), ())), preferred_element_type=F32)


def _dot_split(a, b):
    ah, al = _split_bf16(a)
    bh, bl = _split_bf16(b)
    return _dot(ah, bh) + _dot(al, bh) + _dot(ah, bl)


def _rows(i, size):
    if isinstance(i, int):
        return pl.ds(i * size, size)
    return pl.ds(pl.multiple_of(i * size, size), size)


def _software_pipeline(n_items, stages, static_items=0):
    depth = len(stages)
    n_rounds = n_items + depth - 1

    def static_round(r):
        for k in reversed(range(depth)):
            if 0 <= r - k < n_items:
                stages[k](r - k, (r - k) % 2)

    n_steady = max(n_items - static_items - depth + 1, 0)
    first = min(static_items + depth - 1 + n_steady % 2, n_rounds)
    n_pairs = (n_items - first) // 2 if n_steady else 0
    for r in range(first):
        static_round(r)
    if n_pairs:
        def two_rounds(i, carry):
            for dr in range(2):
                for k in reversed(range(depth)):
                    stages[k](first + 2 * i + dr - k, (first + dr - k) % 2)
            return carry

        lax.fori_loop(0, n_pairs, two_rounds, 0)
    for r in range(first + 2 * n_pairs, n_rounds):
        static_round(r)


def _norm_matmul_body(*refs, scaled_tiles, scale, lowrank):
    if lowrank:
        x_ref, g_ref, w_ref, wl_ref, o_ref, lr_ref, xn_ref = refs
    else:
        x_ref, g_ref, w_ref, o_ref, xn_ref = refs
    j = pl.program_id(1)

    @pl.when(j == 0)
    def _():
        xn = _rms(x_ref[...], g_ref[...])
        xn_ref[...] = xn.astype(BF16)
        if lowrank:
            lr_ref[...] = _dot_split(xn, wl_ref[...])

    acc = _dot(xn_ref[...], w_ref[...])
    if scaled_tiles:
        acc = acc * jnp.where(j < scaled_tiles, scale, 1.0).astype(F32)
    o_ref[...] = acc.astype(o_ref.dtype)


def _norm_matmul(x, g, w, *, scaled_cols=0, scale=1.0, w_lowrank=None):
    t, d = x.shape
    n = w.shape[1]
    tm = min(ROW_TILE, t)
    tn = min(COL_TILE, n)
    assert t % tm == 0 and n % tn == 0 and scaled_cols % tn == 0
    lowrank = w_lowrank is not None
    in_specs = [
        pl.BlockSpec((tm, d), lambda i, j: (i, 0)),
        pl.BlockSpec((1, d), lambda i, j: (0, 0)),
        pl.BlockSpec((d, tn), lambda i, j: (0, j)),
    ]
    args = [x, g.reshape(1, d), w]
    out_shape = [jax.ShapeDtypeStruct((t, n), BF16)]
    out_specs = [pl.BlockSpec((tm, tn), lambda i, j: (i, j))]
    if lowrank:
        in_specs.append(pl.BlockSpec((d, LANES), lambda i, j: (0, 0)))
        args.append(w_lowrank)
        out_shape.append(jax.ShapeDtypeStruct((t, LANES), F32))
        out_specs.append(pl.BlockSpec((tm, LANES), lambda i, j: (i, 0)))
    out = pl.pallas_call(
        functools.partial(_norm_matmul_body, scaled_tiles=scaled_cols // tn, scale=scale, lowrank=lowrank),
        grid=(t // tm, n // tn),
        in_specs=in_specs,
        out_specs=out_specs,
        out_shape=out_shape,
        scratch_shapes=[pltpu.VMEM((tm, d), BF16)],
        compiler_params=pltpu.CompilerParams(
            dimension_semantics=("parallel", "arbitrary"), vmem_limit_bytes=VMEM_LIMIT),
        name="norm_matmul",
    )(*args)
    return out if lowrank else out[0]


def _log_sigmoid(x):
    return jnp.minimum(x, 0.0) - jnp.log1p(jnp.exp(-jnp.abs(x)))


def _gla_tables(c):
    n_lev = c.bit_length() - 1
    assert 1 << n_lev == c
    i = np.arange(c)[:, None]
    t = np.arange(c)[None, :]
    sums = [t <= i, t > i]
    pairs = []
    for lev in range(n_lev):
        s = c >> (lev + 1)
        blk = i // s
        odd = blk % 2 == 1
        sums.append(np.where(odd, (t >= blk * s) & (t <= i), (t > i) & (t < (blk + 1) * s)))
        pairs.append(odd & (t // s == blk - 1))
    pairs.append(i == t)
    pairs = np.stack(pairs).astype(np.float32)
    assert (pairs.sum(0) == np.tril(np.ones((c, c)))).all()
    return np.concatenate(sums).astype(np.float32), pairs


def _gla_body(q_ref, k_ref, v_ref, gate_ref, glr_ref, w2_ref, bias_ref, gn_ref, sums_ref, pairs_ref, o_ref,
              st_ref, stb_ref, *stage_refs, chunk):
    la_refs, e_refs, ql_refs, kl_refs, ab_refs, qb_refs, kd_refs, dec_refs = (
        stage_refs[2 * i:2 * i + 2] for i in range(8))
    c = chunk
    n_lev = pairs_ref.shape[0] - 1
    n_chunks = q_ref.shape[0] // c
    rank = GLA_GATE_RANK

    @pl.when(pl.program_id(2) == 0)
    def _():
        st_ref[...] = jnp.zeros_like(st_ref)
        stb_ref[...] = jnp.zeros_like(stb_ref)

    lane = lax.broadcasted_iota(jnp.int32, (c, LANES), 1)
    lo_lanes = lane // rank == 1

    def gates(i, slot):
        glr = glr_ref[_rows(i, c), :]
        hi = glr.astype(BF16)
        lo = (glr - hi.astype(F32)).astype(BF16)
        gk = _dot(jnp.where(lo_lanes, lo, hi), w2_ref[...]) + bias_ref[...]
        la = _log_sigmoid(gk) * (LOG2E / GLA_GATE_NORMALIZER)
        la_hi, la_lo = _split_bf16(la)
        la_refs[slot][0:c, :] = la_hi
        la_refs[slot][c:2 * c, :] = la_lo

    def decays(i, slot):
        rows = _rows(i, c)
        e = _dot(sums_ref[...], la_refs[slot][...])
        e_refs[slot][...] = e[0:2 * c]
        qf = q_ref[rows, :].astype(F32)
        kf = k_ref[rows, :].astype(F32)
        for lev in range(n_lev):
            w = jnp.exp2(e[(lev + 2) * c:(lev + 3) * c])
            ql_refs[slot][lev] = (qf * w).astype(BF16)
            kl_refs[slot][lev] = (kf * w).astype(BF16)

    def scores(i, slot):
        rows = _rows(i, c)
        q = q_ref[rows, :]
        k = k_ref[rows, :]
        a = pairs_ref[n_lev] * _dot_nt(q, k)
        for lev in range(n_lev):
            a = a + pairs_ref[lev] * _dot_nt(ql_refs[slot][lev], kl_refs[slot][lev])
        ab_refs[slot][...] = a.astype(BF16)
        e = e_refs[slot][...]
        b = e[0:c]
        qb_refs[slot][...] = (q.astype(F32) * jnp.exp2(b)).astype(BF16)
        kd_refs[slot][...] = (k.astype(F32) * jnp.exp2(e[c:2 * c])).astype(BF16)
        dec_refs[slot][...] = jnp.broadcast_to(jnp.exp2(b[c - 1:c, :]), dec_refs[slot].shape)

    def output(i, slot):
        rows = _rows(i, c)
        v = v_ref[rows, :]
        o = _dot_nt(qb_refs[slot][...], stb_ref[...]) + _dot(ab_refs[slot][...], v)
        st = st_ref[...] * dec_refs[slot][0:1, :] + _dot_tn(v, kd_refs[slot][...])
        st_ref[...] = st
        stb_ref[...] = st.astype(BF16)
        g = gate_ref[rows, :].astype(F32)
        o_ref[rows, :] = (_rms(o, gn_ref[...]) * (g * jax.nn.sigmoid(g))).astype(o_ref.dtype)

    _software_pipeline(n_chunks, (gates, decays, scores, output))


def _gla_mixer(proj, glr, w_gk2, b_gk, g_norm, *, batch, seq):
    t = proj.shape[0]
    key_w = w_gk2.shape[1]
    dk = key_w // GLA_HEADS
    mix_w = 2 * key_w
    dv = mix_w // GLA_HEADS
    rows = min(GLA_ROWS, seq)
    nblk = seq // rows
    c = min(GLA_CHUNK, rows)
    assert seq % rows == 0 and rows % c == 0
    k_off = key_w // dk
    v_off = 2 * key_w // dv
    g_off = (2 * key_w + mix_w) // dv
    w_hi, w_lo = _split_bf16(w_gk2)
    w2 = jnp.zeros((LANES, key_w), BF16).at[:3 * GLA_GATE_RANK].set(jnp.concatenate([w_hi, w_hi, w_lo]))
    sums, pairs = _gla_tables(c)
    n_lev = pairs.shape[0] - 1
    sums = jnp.asarray(np.concatenate([sums, sums], axis=1), BF16)
    pairs = jnp.asarray(pairs, F32)

    def rowmap(off):
        return lambda b, h, n: (b * nblk + n, off + h)

    def two(shape, dtype):
        return [pltpu.VMEM(shape, dtype), pltpu.VMEM(shape, dtype)]

    return pl.pallas_call(
        functools.partial(_gla_body, chunk=c),
        grid=(batch, GLA_HEADS, nblk),
        in_specs=[
            pl.BlockSpec((rows, dk), rowmap(0)),
            pl.BlockSpec((rows, dk), rowmap(k_off)),
            pl.BlockSpec((rows, dv), rowmap(v_off)),
            pl.BlockSpec((rows, dv), rowmap(g_off)),
            pl.BlockSpec((rows, LANES), lambda b, h, n: (b * nblk + n, 0)),
            pl.BlockSpec((LANES, dk), lambda b, h, n: (0, h)),
            pl.BlockSpec((1, dk), lambda b, h, n: (0, h)),
            pl.BlockSpec((1, dv), lambda b, h, n: (0, 0)),
            pl.BlockSpec(sums.shape, lambda b, h, n: (0, 0)),
            pl.BlockSpec(pairs.shape, lambda b, h, n: (0, 0, 0)),
        ],
        out_specs=pl.BlockSpec((rows, dv), rowmap(0)),
        out_shape=jax.ShapeDtypeStruct((t, mix_w), BF16),
        scratch_shapes=(
            [pltpu.VMEM((dv, dk), F32), pltpu.VMEM((dv, dk), BF16)]
            + two((2 * c, dk), BF16)
            + two((2 * c, dk), F32)
            + two((n_lev, c, dk), BF16) + two((n_lev, c, dk), BF16)
            + two((c, c), BF16)
            + two((c, dk), BF16) + two((c, dk), BF16)
            + two((8, dk), F32)),
        compiler_params=pltpu.CompilerParams(
            dimension_semantics=("parallel", "parallel", "arbitrary"), vmem_limit_bytes=VMEM_LIMIT),
        name="gla_mixer",
    )(proj, proj, proj, proj, glr, w2, b_gk.reshape(1, key_w), g_norm.reshape(1, dv), sums, pairs)


def _diff_attn_body(qt_ref, kt_ref, q_ref, k_ref, v_ref, gate_ref, lam_ref, gn_ref, o_ref, s0_ref, s1_ref,
                    mx0_ref, mx1_ref, p0_ref, p1_ref, al0_ref, al1_ref, m_ref, l_ref, acc_ref,
                    *, tile, tiles, lambda_init):
    d = DIFF_HEAD_DIM
    n_tiles = q_ref.shape[0] // tile
    s_refs, mx_refs = (s0_ref, s1_ref), (mx0_ref, mx1_ref)
    p_refs, al_refs = (p0_ref, p1_ref), (al0_ref, al1_ref)
    reps = tile // LANES
    wide = 2 * d // LANES

    def item(n):
        return tiles[n] if isinstance(n, int) else (qt_ref[n], kt_ref[n])

    def scores(n, slot):
        qt, kt = item(n)
        q_rows, k_rows = _rows(qt, tile), _rows(kt, tile)
        for t in range(2):
            cols = slice(t * d, (t + 1) * d)
            s = _dot_nt(q_ref[q_rows, cols], k_ref[k_rows, cols])
            s_refs[slot][t] = s
            mx_refs[slot][t] = jnp.broadcast_to(jnp.max(s, axis=1, keepdims=True), (tile, LANES))

    def weights(n, slot):
        first = isinstance(n, int) and n < n_tiles
        q_rows = _rows(item(n)[0], tile)
        for t in range(2):
            s = s_refs[slot][t]
            if first:
                row = lax.broadcasted_iota(jnp.int32, s.shape, 0)
                col = lax.broadcasted_iota(jnp.int32, s.shape, 1)
                s = jnp.where(col <= row, s, -1e30)
                m_new = jnp.broadcast_to(jnp.max(s, axis=1, keepdims=True), (tile, LANES))
            else:
                m_old = m_ref[t, q_rows, :]
                m_new = jnp.maximum(m_old, mx_refs[slot][t])
                alpha = jnp.exp2(m_old - m_new)
                al_refs[slot][t] = alpha
            p = jnp.exp2(s - pltpu.repeat(m_new, reps, axis=1))
            p_sum = jnp.sum(p, axis=1, keepdims=True)
            if first:
                l_ref[t, q_rows, :] = jnp.broadcast_to(p_sum, (tile, LANES))
            else:
                l_ref[t, q_rows, :] = alpha * l_ref[t, q_rows, :] + p_sum
            m_ref[t, q_rows, :] = m_new
            p_refs[slot][t] = p.astype(BF16)

    def values(n, slot):
        first = isinstance(n, int) and n < n_tiles
        qt, kt = item(n)
        q_rows = _rows(qt, tile)
        vb = v_ref[_rows(kt, tile), :]
        for t in range(2):
            pv = _dot(p_refs[slot][t], vb)
            if first:
                acc_ref[t, q_rows, :] = pv
            else:
                acc_ref[t, q_rows, :] = acc_ref[t, q_rows, :] * pltpu.repeat(al_refs[slot][t], wide, axis=1) + pv

    _software_pipeline(len(tiles), (scores, weights, values), static_items=n_tiles)

    lam = lam_ref[...]
    lam_full = (jnp.exp(jnp.sum(lam[0:1] * lam[1:2], axis=1, keepdims=True))
                - jnp.exp(jnp.sum(lam[2:3] * lam[3:4], axis=1, keepdims=True)) + lambda_init)

    def finish(qt, carry):
        rows = _rows(qt, tile)
        o1 = acc_ref[0, rows, :] / pltpu.repeat(l_ref[0, rows, :], wide, axis=1)
        o2 = acc_ref[1, rows, :] / pltpu.repeat(l_ref[1, rows, :], wide, axis=1)
        o = _rms(o1 - lam_full * o2, gn_ref[...]) * (1.0 - lambda_init)
        g = gate_ref[rows, :].astype(F32)
        o_ref[rows, :] = (o * (g * jax.nn.sigmoid(g))).astype(o_ref.dtype)
        return carry

    lax.fori_loop(0, n_tiles, finish, 0)


def _diff_mixer(qg, kv, lam, g_norm, lambda_init, *, batch, seq):
    t = qg.shape[0]
    mix_w = qg.shape[1] // 2
    hw = 2 * DIFF_HEAD_DIM
    heads = mix_w // hw
    tile = min(ATTN_TILE, seq)
    nq = seq // tile
    assert seq % tile == 0
    tiles = tuple((i, i) for i in range(nq)) + tuple((i, j) for i in range(nq) for j in range(i))
    qt_tab = jnp.asarray([qt for qt, _ in tiles], jnp.int32)
    kt_tab = jnp.asarray([kt for _, kt in tiles], jnp.int32)

    def two(shape, dtype):
        return [pltpu.VMEM(shape, dtype), pltpu.VMEM(shape, dtype)]

    return pl.pallas_call(
        functools.partial(_diff_attn_body, tile=tile, tiles=tiles, lambda_init=lambda_init),
        grid_spec=pltpu.PrefetchScalarGridSpec(
            num_scalar_prefetch=2,
            grid=(batch, heads),
            in_specs=[
                pl.BlockSpec((seq, hw), lambda b, h, *_: (b, h)),
                pl.BlockSpec((seq, hw), lambda b, h, *_: (b, h)),
                pl.BlockSpec((seq, hw), lambda b, h, *_: (b, heads + h)),
                pl.BlockSpec((seq, hw), lambda b, h, *_: (b, heads + h)),
                pl.BlockSpec((4, DIFF_HEAD_DIM), lambda b, h, *_: (0, 0)),
                pl.BlockSpec((1, hw), lambda b, h, *_: (0, 0)),
            ],
            out_specs=pl.BlockSpec((seq, hw), lambda b, h, *_: (b, h)),
            scratch_shapes=(
                two((2, tile, tile), F32)
                + two((2, tile, LANES), F32)
                + two((2, tile, tile), BF16)
                + two((2, tile, LANES), F32)
                + [pltpu.VMEM((2, seq, LANES), F32),
                   pltpu.VMEM((2, seq, LANES), F32),
                   pltpu.VMEM((2, seq, hw), F32)]),
        ),
        out_shape=jax.ShapeDtypeStruct((t, mix_w), BF16),
        compiler_params=pltpu.CompilerParams(
            dimension_semantics=("parallel", "parallel"), vmem_limit_bytes=VMEM_LIMIT),
        name="diff_attn",
    )(qt_tab, kt_tab, qg, kv, kv, qg, lam, g_norm.reshape(1, hw))


def _out_ple_body(o_ref, h_ref, p_ref, wo_ref, pn_ref, wg_ref, wp_ref, fn_ref, out_ref, *, final):
    hm = h_ref[...] + _dot(o_ref[...], wo_ref[...])
    hn = _rms(hm, pn_ref[...]).astype(BF16)
    gate = jax.nn.sigmoid(_dot(hn, wg_ref[...]))
    emb = _dot(p_ref[...].astype(BF16), wp_ref[...])
    h2 = hm + gate * emb
    if final:
        h2 = _rms(h2, fn_ref[...])
    out_ref[...] = h2


def _out_ple(o, h, p, w_out, ple_norm, w_gate, w_proj, final_norm, *, final):
    t, d = h.shape
    mix_w = o.shape[1]
    pd = p.shape[1]
    tm = min(OUT_ROW_TILE, t)
    assert t % tm == 0

    def const(shape):
        return pl.BlockSpec(shape, lambda i: (0, 0))

    return pl.pallas_call(
        functools.partial(_out_ple_body, final=final),
        grid=(t // tm,),
        in_specs=[
            pl.BlockSpec((tm, mix_w), lambda i: (i, 0)),
            pl.BlockSpec((tm, d), lambda i: (i, 0)),
            pl.BlockSpec((tm, pd), lambda i: (i, 0)),
            const((mix_w, d)),
            const((1, d)),
            const((d, d)),
            const((pd, d)),
            const((1, d)),
        ],
        out_specs=pl.BlockSpec((tm, d), lambda i: (i, 0)),
        out_shape=jax.ShapeDtypeStruct((t, d), F32),
        compiler_params=pltpu.CompilerParams(
            dimension_semantics=("parallel",), vmem_limit_bytes=VMEM_LIMIT),
        name="out_ple",
    )(o, h, p, w_out, ple_norm.reshape(1, d), w_gate, w_proj, final_norm.reshape(1, d))


def kernel(x, p, norm_mix, gla_w_in, gla_w_gk2, gla_b_gk, gla_norm, gla_w_out, kv_norm, w_kv, diff_w_in,
           diff_lambda, diff_norm, diff_w_out, ple_norm, ple_w_gate, ple_w_proj, final_norm):
    batch, seq, d = x.shape
    depth = p.shape[0]
    n_gla = gla_w_in.shape[0]
    t = batch * seq
    key_w = gla_w_gk2.shape[2]
    main_cols = gla_w_in.shape[2] - GLA_GATE_RANK

    h = x.reshape(t, d)
    kv = None
    for i in range(depth):
        if i < n_gla:
            w_main = gla_w_in[i, :, :main_cols].astype(BF16)
            w_gate = gla_w_in[i, :, main_cols:]
            w_lr = jnp.zeros((d, LANES), F32).at[:, :3 * GLA_GATE_RANK].set(jnp.concatenate([w_gate] * 3, axis=1))
            proj, glr = _norm_matmul(h, norm_mix[i], w_main, scaled_cols=key_w,
                                     scale=(key_w // GLA_HEADS) ** -0.5, w_lowrank=w_lr)
            o = _gla_mixer(proj, glr, gla_w_gk2[i], gla_b_gk[i], gla_norm[i], batch=batch, seq=seq)
            w_out = gla_w_out[i]
        else:
            j = i - n_gla
            lambda_init = 0.8 - 0.6 * math.exp(-0.3 * i)
            qg = _norm_matmul(h, norm_mix[i], diff_w_in[j].astype(BF16), scaled_cols=diff_w_in.shape[2] // 2,
                              scale=DIFF_HEAD_DIM ** -0.5 * LOG2E)
            o = _diff_mixer(qg, kv, diff_lambda[j], diff_norm[j], lambda_init, batch=batch, seq=seq)
            w_out = diff_w_out[j]
        h = _out_ple(o, h, p[i].reshape(t, p.shape[3]), w_out.astype(BF16), ple_norm[i],
                     ple_w_gate[i].astype(BF16), ple_w_proj[i].astype(BF16), final_norm,
                     final=(i == depth - 1))
        if i == n_gla - 1:
            kv = _norm_matmul(h, kv_norm, w_kv.astype(BF16))
    return h.reshape(batch, seq, d)
```

```python
import functools
import math

import jax
import jax.numpy as jnp
import numpy as np
from jax import lax
from jax.experimental import pallas as pl
from jax.experimental.pallas import tpu as pltpu

F32 = jnp.float32
BF16 = jnp.bfloat16
EPS = 1e-6
LOG2E = math.log2(math.e)

EXPAND = 2
PLE_DIM = 256
GLA_HEADS = 4
GLA_GATE_RANK = 16
GLA_GATE_NORMALIZER = 16.0
DIFF_HEAD_DIM = 128

LANES = 128
VMEM_LIMIT = 56 * 1024 * 1024

GLA_CHUNK = 128
GLA_ROWS = 2048
GLA_STAGE_ORDER = (0, 3, 1, 2)
GLA_ROUNDS_PER_TRIP = 2
ATTN_TILE = 512
ATTN_STAGE_ORDER = (2, 1, 0)
ATTN_ROUNDS_PER_TRIP = 2
ROW_TILE = 512
OUT_ROW_TILE = 512
COL_TILE = 1024


def _rms(x, g):
    return x * lax.rsqrt(jnp.mean(x * x, axis=-1, keepdims=True) + EPS) * g


def _split_bf16(x):
    hi = x.astype(BF16)
    lo = (x - hi.astype(F32)).astype(BF16)
    return hi, lo


def _dot(a, b):
    return jnp.dot(a, b, preferred_element_type=F32)


def _dot_nt(a, b):
    return lax.dot_general(a, b, (((1,), (1,)), ((), ())), preferred_element_type=F32)


def _dot_tn(a, b):
    return lax.dot_general(a, b, (((0,), (0,)), ((), ())), preferred_element_type=F32)


def _dot_split(a, b):
    ah, al = _split_bf16(a)
    bh, bl = _split_bf16(b)
    return _dot(ah, bh) + _dot(al, bh) + _dot(ah, bl)


def _rows(i, size):
    if isinstance(i, int):
        return pl.ds(i * size, size)
    return pl.ds(pl.multiple_of(i * size, size), size)


def _software_pipeline(n_items, stages, order, rounds_per_trip, static_items=0):
    depth = len(stages)
    n_rounds = n_items + depth - 1
    assert sorted(order) == list(range(depth)) and rounds_per_trip % 2 == 0

    def static_round(r):
        for k in order:
            if 0 <= r - k < n_items:
                stages[k](r - k, (r - k) % 2)

    n_steady = max(n_items - static_items - depth + 1, 0)
    n_trips = n_steady // rounds_per_trip
    first = n_items - n_trips * rounds_per_trip if n_trips else n_rounds
    for r in range(first):
        static_round(r)
    if n_trips:
        def trip(i, carry):
            for dr in range(rounds_per_trip):
                for k in order:
                    stages[k](first + rounds_per_trip * i + dr - k, (first + dr - k) % 2)
            return carry

        lax.fori_loop(0, n_trips, trip, 0)
    for r in range(n_items if n_trips else n_rounds, n_rounds):
        static_round(r)


def _norm_matmul_body(*refs, col_tile, scaled_cols, scale, lowrank):
    if lowrank:
        x_ref, g_ref, w_ref, wl_ref, o_ref, lr_ref = refs
    else:
        x_ref, g_ref, w_ref, o_ref = refs
    xn = _rms(x_ref[...], g_ref[...])
    xh = xn.astype(BF16)
    if lowrank:
        xl = (xn - xh.astype(F32)).astype(BF16)
        both = _dot(xh, wl_ref[...])
        lr_ref[...] = both[:, :LANES] + both[:, LANES:] + _dot(xl, wl_ref[:, :LANES])
    for c0 in range(0, o_ref.shape[1], col_tile):
        acc = _dot(xh, w_ref[:, c0:c0 + col_tile])
        if c0 < scaled_cols:
            acc = acc * scale
        o_ref[:, c0:c0 + col_tile] = acc.astype(o_ref.dtype)


def _norm_matmul(x, g, w, *, scaled_cols=0, scale=1.0, w_lowrank=None):
    t, d = x.shape
    n = w.shape[1]
    tm = min(ROW_TILE, t)
    tn = min(COL_TILE, n)
    assert t % tm == 0 and n % tn == 0 and scaled_cols % tn == 0
    lowrank = w_lowrank is not None

    def resident(shape):
        return pl.BlockSpec(shape, lambda i: (0, 0))

    in_specs = [pl.BlockSpec((tm, d), lambda i: (i, 0)), resident((1, d)), resident((d, n))]
    args = [x, g.reshape(1, d), w]
    out_shape = [jax.ShapeDtypeStruct((t, n), BF16)]
    out_specs = [pl.BlockSpec((tm, n), lambda i: (i, 0))]
    if lowrank:
        in_specs.append(resident((d, 2 * LANES)))
        args.append(w_lowrank)
        out_shape.append(jax.ShapeDtypeStruct((t, LANES), F32))
        out_specs.append(pl.BlockSpec((tm, LANES), lambda i: (i, 0)))
    out = pl.pallas_call(
        functools.partial(_norm_matmul_body, col_tile=tn, scaled_cols=scaled_cols, scale=scale, lowrank=lowrank),
        grid=(t // tm,),
        in_specs=in_specs,
        out_specs=out_specs,
        out_shape=out_shape,
        compiler_params=pltpu.CompilerParams(dimension_semantics=("parallel",), vmem_limit_bytes=VMEM_LIMIT),
        name="norm_matmul",
    )(*args)
    return out if lowrank else out[0]


def _log_sigmoid(x):
    return jnp.minimum(x, 0.0) - jnp.log(1.0 + jnp.exp(-jnp.abs(x)))


def _gla_tables(c):
    n_lev = c.bit_length() - 1
    assert 1 << n_lev == c
    i = np.arange(c)[:, None]
    t = np.arange(c)[None, :]
    sums = [t <= i, t > i]
    pairs = []
    for lev in range(n_lev):
        s = c >> (lev + 1)
        blk = i // s
        odd = blk % 2 == 1
        sums.append(np.where(odd, (t >= blk * s) & (t <= i), (t > i) & (t < (blk + 1) * s)))
        pairs.append(odd & (t // s == blk - 1))
    pairs.append(i == t)
    pairs = np.stack(pairs).astype(np.float32)
    assert (pairs.sum(0) == np.tril(np.ones((c, c)))).all()
    return np.concatenate(sums).astype(np.float32), pairs


def _gla_body(q_ref, k_ref, v_ref, gate_ref, glr_ref, w2_ref, bias_ref, gn_ref, sums_ref, pairs_ref, o_ref,
              st_ref, stb_ref, *stage_refs, chunk):
    la_refs, e_refs, ql_refs, kl_refs, ab_refs, qb_refs, kd_refs, dec_refs = (
        stage_refs[2 * i:2 * i + 2] for i in range(8))
    c = chunk
    n_lev = pairs_ref.shape[0] - 1
    n_chunks = q_ref.shape[0] // c
    rank = GLA_GATE_RANK

    @pl.when(pl.program_id(2) == 0)
    def _():
        st_ref[...] = jnp.zeros_like(st_ref)
        stb_ref[...] = jnp.zeros_like(stb_ref)

    lane = lax.broadcasted_iota(jnp.int32, (c, LANES), 1)
    lo_lanes = lane // rank == 1

    def gates(i, slot):
        glr = glr_ref[_rows(i, c), :]
        hi = glr.astype(BF16)
        lo = (glr - hi.astype(F32)).astype(BF16)
        gk = _dot(jnp.where(lo_lanes, lo, hi), w2_ref[...]) + bias_ref[...]
        la = _log_sigmoid(gk) * (LOG2E / GLA_GATE_NORMALIZER)
        la_hi, la_lo = _split_bf16(la)
        la_refs[slot][0:c, :] = la_hi
        la_refs[slot][c:2 * c, :] = la_lo

    def decays(i, slot):
        rows = _rows(i, c)
        e = _dot(sums_ref[...], la_refs[slot][...])
        e_refs[slot][...] = e[0:2 * c]
        qf = q_ref[rows, :].astype(F32)
        kf = k_ref[rows, :].astype(F32)
        for lev in range(n_lev):
            w = jnp.exp2(e[(lev + 2) * c:(lev + 3) * c])
            ql_refs[slot][lev] = (qf * w).astype(BF16)
            kl_refs[slot][lev] = (kf * w).astype(BF16)

    def scores(i, slot):
        rows = _rows(i, c)
        q = q_ref[rows, :]
        k = k_ref[rows, :]
        a = pairs_ref[n_lev] * _dot_nt(q, k)
        for lev in range(n_lev):
            a = a + pairs_ref[lev] * _dot_nt(ql_refs[slot][lev], kl_refs[slot][lev])
        ab_refs[slot][...] = a.astype(BF16)
        e = e_refs[slot][...]
        b = e[0:c]
        qb_refs[slot][...] = (q.astype(F32) * jnp.exp2(b)).astype(BF16)
        kd_refs[slot][...] = (k.astype(F32) * jnp.exp2(e[c:2 * c])).astype(BF16)
        dec_refs[slot][...] = jnp.broadcast_to(jnp.exp2(b[c - 1:c, :]), dec_refs[slot].shape)

    def output(i, slot):
        rows = _rows(i, c)
        v = v_ref[rows, :]
        o = _dot_nt(qb_refs[slot][...], stb_ref[...]) + _dot(ab_refs[slot][...], v)
        st = st_ref[...] * dec_refs[slot][0:1, :] + _dot_tn(v, kd_refs[slot][...])
        st_ref[...] = st
        stb_ref[...] = st.astype(BF16)
        g = gate_ref[rows, :].astype(F32)
        o_ref[rows, :] = (_rms(o, gn_ref[...]) * (g * jax.nn.sigmoid(g))).astype(o_ref.dtype)

    _software_pipeline(n_chunks, (gates, decays, scores, output), GLA_STAGE_ORDER, GLA_ROUNDS_PER_TRIP)


def _gla_mixer(proj, glr, w_gk2, b_gk, g_norm, *, batch, seq):
    t = proj.shape[0]
    key_w = w_gk2.shape[1]
    dk = key_w // GLA_HEADS
    mix_w = 2 * key_w
    dv = mix_w // GLA_HEADS
    rows = min(GLA_ROWS, seq)
    nblk = seq // rows
    c = min(GLA_CHUNK, rows)
    assert seq % rows == 0 and rows % c == 0
    k_off = key_w // dk
    v_off = 2 * key_w // dv
    g_off = (2 * key_w + mix_w) // dv
    w_hi, w_lo = _split_bf16(w_gk2)
    w2 = jnp.zeros((LANES, key_w), BF16).at[:3 * GLA_GATE_RANK].set(jnp.concatenate([w_hi, w_hi, w_lo]))
    sums, pairs = _gla_tables(c)
    n_lev = pairs.shape[0] - 1
    sums = jnp.asarray(np.concatenate([sums, sums], axis=1), BF16)
    pairs = jnp.asarray(pairs, F32)

    def rowmap(off):
        return lambda b, h, n: (b * nblk + n, off + h)

    def two(shape, dtype):
        return [pltpu.VMEM(shape, dtype), pltpu.VMEM(shape, dtype)]

    return pl.pallas_call(
        functools.partial(_gla_body, chunk=c),
        grid=(batch, GLA_HEADS, nblk),
        in_specs=[
            pl.BlockSpec((rows, dk), rowmap(0)),
            pl.BlockSpec((rows, dk), rowmap(k_off)),
            pl.BlockSpec((rows, dv), rowmap(v_off)),
            pl.BlockSpec((rows, dv), rowmap(g_off)),
            pl.BlockSpec((rows, LANES), lambda b, h, n: (b * nblk + n, 0)),
            pl.BlockSpec((LANES, dk), lambda b, h, n: (0, h)),
            pl.BlockSpec((1, dk), lambda b, h, n: (0, h)),
            pl.BlockSpec((1, dv), lambda b, h, n: (0, 0)),
            pl.BlockSpec(sums.shape, lambda b, h, n: (0, 0)),
            pl.BlockSpec(pairs.shape, lambda b, h, n: (0, 0, 0)),
        ],
        out_specs=pl.BlockSpec((rows, dv), rowmap(0)),
        out_shape=jax.ShapeDtypeStruct((t, mix_w), BF16),
        scratch_shapes=(
            [pltpu.VMEM((dv, dk), F32), pltpu.VMEM((dv, dk), BF16)]
            + two((2 * c, dk), BF16)
            + two((2 * c, dk), F32)
            + two((n_lev, c, dk), BF16) + two((n_lev, c, dk), BF16)
            + two((c, c), BF16)
            + two((c, dk), BF16) + two((c, dk), BF16)
            + two((8, dk), F32)),
        compiler_params=pltpu.CompilerParams(
            dimension_semantics=("parallel", "parallel", "arbitrary"), vmem_limit_bytes=VMEM_LIMIT),
        name="gla_mixer",
    )(proj, proj, proj, proj, glr, w2, b_gk.reshape(1, key_w), g_norm.reshape(1, dv), sums, pairs)


def _diff_attn_body(qt_ref, kt_ref, q_ref, k_ref, v_ref, gate_ref, lam_ref, gn_ref, o_ref, s0_ref, s1_ref,
                    mx0_ref, mx1_ref, p0_ref, p1_ref, al0_ref, al1_ref, m_ref, l_ref, acc_ref,
                    *, tile, tiles, lambda_init):
    d = DIFF_HEAD_DIM
    n_tiles = q_ref.shape[0] // tile
    s_refs, mx_refs = (s0_ref, s1_ref), (mx0_ref, mx1_ref)
    p_refs, al_refs = (p0_ref, p1_ref), (al0_ref, al1_ref)
    reps = tile // LANES
    wide = 2 * d // LANES

    def item(n):
        return tiles[n] if isinstance(n, int) else (qt_ref[n], kt_ref[n])

    def is_diagonal(n):
        return isinstance(n, int) and n < n_tiles

    half = tile // 2
    bands = ((0, half, half), (half, tile, tile))

    def scores(n, slot):
        qt, kt = item(n)
        for t in range(2):
            cols = slice(t * d, (t + 1) * d)
            if is_diagonal(n):
                for r0, r1, nk in bands:
                    s_refs[slot][t, r0:r1, 0:nk] = _dot_nt(q_ref[pl.ds(qt * tile + r0, r1 - r0), cols],
                                                           k_ref[pl.ds(kt * tile, nk), cols])
            else:
                s = _dot_nt(q_ref[_rows(qt, tile), cols], k_ref[_rows(kt, tile), cols])
                s_refs[slot][t] = s
                mx_refs[slot][t] = jnp.broadcast_to(jnp.max(s, axis=1, keepdims=True), (tile, LANES))

    def weights(n, slot):
        qt = item(n)[0]
        for t in range(2):
            if is_diagonal(n):
                for r0, r1, nk in bands:
                    q_rows = pl.ds(qt * tile + r0, r1 - r0)
                    s = s_refs[slot][t, r0:r1, 0:nk]
                    row = lax.broadcasted_iota(jnp.int32, s.shape, 0) + r0
                    col = lax.broadcasted_iota(jnp.int32, s.shape, 1)
                    s = jnp.where(col <= row, s, -1e30)
                    m_new = jnp.broadcast_to(jnp.max(s, axis=1, keepdims=True), (r1 - r0, LANES))
                    p = jnp.exp2(s - pltpu.repeat(m_new, nk // LANES, axis=1))
                    l_ref[t, q_rows, :] = jnp.broadcast_to(jnp.sum(p, axis=1, keepdims=True), (r1 - r0, LANES))
                    m_ref[t, q_rows, :] = m_new
                    p_refs[slot][t, r0:r1, 0:nk] = p.astype(BF16)
            else:
                q_rows = _rows(qt, tile)
                m_old = m_ref[t, q_rows, :]
                m_new = jnp.maximum(m_old, mx_refs[slot][t])
                alpha = jnp.exp2(m_old - m_new)
                p = jnp.exp2(s_refs[slot][t] - pltpu.repeat(m_new, reps, axis=1))
                l_ref[t, q_rows, :] = alpha * l_ref[t, q_rows, :] + jnp.sum(p, axis=1, keepdims=True)
                m_ref[t, q_rows, :] = m_new
                al_refs[slot][t] = alpha
                p_refs[slot][t] = p.astype(BF16)

    def values(n, slot):
        qt, kt = item(n)
        for t in range(2):
            if is_diagonal(n):
                for r0, r1, nk in bands:
                    acc_ref[t, pl.ds(qt * tile + r0, r1 - r0), :] = _dot(p_refs[slot][t, r0:r1, 0:nk],
                                                                        v_ref[pl.ds(kt * tile, nk), :])
            else:
                q_rows = _rows(qt, tile)
                acc_ref[t, q_rows, :] = (acc_ref[t, q_rows, :] * pltpu.repeat(al_refs[slot][t], wide, axis=1)
                                         + _dot(p_refs[slot][t], v_ref[_rows(kt, tile), :]))

    _software_pipeline(len(tiles), (scores, weights, values), ATTN_STAGE_ORDER, ATTN_ROUNDS_PER_TRIP,
                       static_items=n_tiles)

    lam = lam_ref[...]
    lam_full = (jnp.exp(jnp.sum(lam[0:1] * lam[1:2], axis=1, keepdims=True))
                - jnp.exp(jnp.sum(lam[2:3] * lam[3:4], axis=1, keepdims=True)) + lambda_init)

    gain = gn_ref[...] * (1.0 - lambda_init)

    def finish(qt, carry):
        rows = _rows(qt, tile)
        o1 = acc_ref[0, rows, :] * pltpu.repeat(1.0 / l_ref[0, rows, :], wide, axis=1)
        o2 = acc_ref[1, rows, :] * pltpu.repeat(lam_full / l_ref[1, rows, :], wide, axis=1)
        o = _rms(o1 - o2, gain)
        g = gate_ref[rows, :].astype(F32)
        o_ref[rows, :] = (o * (g * jax.nn.sigmoid(g))).astype(o_ref.dtype)
        return carry

    lax.fori_loop(0, n_tiles, finish, 0)


def _diff_mixer(qg, kv, lam, g_norm, lambda_init, *, batch, seq):
    t = qg.shape[0]
    mix_w = qg.shape[1] // 2
    hw = 2 * DIFF_HEAD_DIM
    heads = mix_w // hw
    tile = min(ATTN_TILE, seq)
    nq = seq // tile
    assert seq % tile == 0
    tiles = tuple((i, i) for i in range(nq)) + tuple((i, j) for i in range(nq) for j in range(i))
    qt_tab = jnp.asarray([qt for qt, _ in tiles], jnp.int32)
    kt_tab = jnp.asarray([kt for _, kt in tiles], jnp.int32)

    def two(shape, dtype):
        return [pltpu.VMEM(shape, dtype), pltpu.VMEM(shape, dtype)]

    return pl.pallas_call(
        functools.partial(_diff_attn_body, tile=tile, tiles=tiles, lambda_init=lambda_init),
        grid_spec=pltpu.PrefetchScalarGridSpec(
            num_scalar_prefetch=2,
            grid=(batch, heads),
            in_specs=[
                pl.BlockSpec((seq, hw), lambda b, h, *_: (b, h)),
                pl.BlockSpec((seq, hw), lambda b, h, *_: (b, h)),
                pl.BlockSpec((seq, hw), lambda b, h, *_: (b, heads + h)),
                pl.BlockSpec((seq, hw), lambda b, h, *_: (b, heads + h)),
                pl.BlockSpec((4, DIFF_HEAD_DIM), lambda b, h, *_: (0, 0)),
                pl.BlockSpec((1, hw), lambda b, h, *_: (0, 0)),
            ],
            out_specs=pl.BlockSpec((seq, hw), lambda b, h, *_: (b, h)),
            scratch_shapes=(
                two((2, tile, tile), F32)
                + two((2, tile, LANES), F32)
                + two((2, tile, tile), BF16)
                + two((2, tile, LANES), F32)
                + [pltpu.VMEM((2, seq, LANES), F32),
                   pltpu.VMEM((2, seq, LANES), F32),
                   pltpu.VMEM((2, seq, hw), F32)]),
        ),
        out_shape=jax.ShapeDtypeStruct((t, mix_w), BF16),
        compiler_params=pltpu.CompilerParams(
            dimension_semantics=("parallel", "parallel"), vmem_limit_bytes=VMEM_LIMIT),
        name="diff_attn",
    )(qt_tab, kt_tab, qg, kv, kv, qg, lam, g_norm.reshape(1, hw))


def _out_ple_body(o_ref, h_ref, p_ref, wo_ref, pn_ref, wg_ref, wp_ref, fn_ref, out_ref, *, final):
    hm = h_ref[...] + _dot(o_ref[...], wo_ref[...])
    hn = _rms(hm, pn_ref[...]).astype(BF16)
    gate = jax.nn.sigmoid(_dot(hn, wg_ref[...]))
    emb = _dot(p_ref[...].astype(BF16), wp_ref[...])
    h2 = hm + gate * emb
    if final:
        h2 = _rms(h2, fn_ref[...])
    out_ref[...] = h2


def _out_ple(o, h, p, w_out, ple_norm, w_gate, w_proj, final_norm, *, final):
    t, d = h.shape
    mix_w = o.shape[1]
    pd = p.shape[1]
    tm = min(OUT_ROW_TILE, t)
    assert t % tm == 0

    def const(shape):
        return pl.BlockSpec(shape, lambda i: (0, 0))

    return pl.pallas_call(
        functools.partial(_out_ple_body, final=final),
        grid=(t // tm,),
        in_specs=[
            pl.BlockSpec((tm, mix_w), lambda i: (i, 0)),
            pl.BlockSpec((tm, d), lambda i: (i, 0)),
            pl.BlockSpec((tm, pd), lambda i: (i, 0)),
            const((mix_w, d)),
            const((1, d)),
            const((d, d)),
            const((pd, d)),
            const((1, d)),
        ],
        out_specs=pl.BlockSpec((tm, d), lambda i: (i, 0)),
        out_shape=jax.ShapeDtypeStruct((t, d), F32),
        compiler_params=pltpu.CompilerParams(
            dimension_semantics=("parallel",), vmem_limit_bytes=VMEM_LIMIT),
        name="out_ple",
    )(o, h, p, w_out, ple_norm.reshape(1, d), w_gate, w_proj, final_norm.reshape(1, d))


def kernel(x, p, norm_mix, gla_w_in, gla_w_gk2, gla_b_gk, gla_norm, gla_w_out, kv_norm, w_kv, diff_w_in,
           diff_lambda, diff_norm, diff_w_out, ple_norm, ple_w_gate, ple_w_proj, final_norm):
    batch, seq, d = x.shape
    depth = p.shape[0]
    n_gla = gla_w_in.shape[0]
    t = batch * seq
    key_w = gla_w_gk2.shape[2]
    main_cols = gla_w_in.shape[2] - GLA_GATE_RANK

    h = x.reshape(t, d)
    kv = None
    for i in range(depth):
        if i < n_gla:
            w_main = gla_w_in[i, :, :main_cols].astype(BF16)
            w_gate = gla_w_in[i, :, main_cols:]
            w_lr = jnp.zeros((d, LANES), F32).at[:, :3 * GLA_GATE_RANK].set(jnp.concatenate([w_gate] * 3, axis=1))
            w_lr = jnp.concatenate(_split_bf16(w_lr), axis=1)
            proj, glr = _norm_matmul(h, norm_mix[i], w_main, scaled_cols=key_w,
                                     scale=(key_w // GLA_HEADS) ** -0.5, w_lowrank=w_lr)
            o = _gla_mixer(proj, glr, gla_w_gk2[i], gla_b_gk[i], gla_norm[i], batch=batch, seq=seq)
            w_out = gla_w_out[i]
        else:
            j = i - n_gla
            lambda_init = 0.8 - 0.6 * math.exp(-0.3 * i)
            qg = _norm_matmul(h, norm_mix[i], diff_w_in[j].astype(BF16), scaled_cols=diff_w_in.shape[2] // 2,
                              scale=DIFF_HEAD_DIM ** -0.5 * LOG2E)
            o = _diff_mixer(qg, kv, diff_lambda[j], diff_norm[j], lambda_init, batch=batch, seq=seq)
            w_out = diff_w_out[j]
        h = _out_ple(o, h, p[i].reshape(t, p.shape[3]), w_out.astype(BF16), ple_norm[i],
                     ple_w_gate[i].astype(BF16), ple_w_proj[i].astype(BF16), final_norm,
                     final=(i == depth - 1))
        if i == n_gla - 1:
            kv = _norm_matmul(h, kv_norm, w_kv.astype(BF16))
    return h.reshape(batch, seq, d)
```

```python
import functools
import math

import jax
import jax.numpy as jnp
import numpy as np
from jax import lax
from jax.experimental import pallas as pl
from jax.experimental.pallas import tpu as pltpu

F32 = jnp.float32
BF16 = jnp.bfloat16
EPS = 1e-6
LOG2E = math.log2(math.e)

EXPAND = 2
PLE_DIM = 256
GLA_HEADS = 4
GLA_GATE_RANK = 16
GLA_GATE_NORMALIZER = 16.0
DIFF_HEAD_DIM = 128

LANES = 128
VMEM_LIMIT = 56 * 1024 * 1024

GLA_CHUNK = 128
GLA_ROWS = 4096
GLA_STAGE_ORDER = (0, 3, 1, 2)
GLA_ROUNDS_PER_TRIP = 4
ATTN_TILE = 512
ATTN_STAGE_ORDER = (2, 1, 0)
ATTN_ROUNDS_PER_TRIP = 4
ROW_TILE = 512
OUT_ROW_TILE = 512
COL_TILE = 1024


def _rms(x, g):
    return x * lax.rsqrt(jnp.mean(x * x, axis=-1, keepdims=True) + EPS) * g


def _split_bf16(x):
    hi = x.astype(BF16)
    lo = (x - hi.astype(F32)).astype(BF16)
    return hi, lo


def _dot(a, b):
    return jnp.dot(a, b, preferred_element_type=F32)


def _dot_nt(a, b):
    return lax.dot_general(a, b, (((1,), (1,)), ((), ())), preferred_element_type=F32)


def _dot_tn(a, b):
    return lax.dot_general(a, b, (((0,), (0,)), ((), ())), preferred_element_type=F32)


def _dot_split(a, b):
    ah, al = _split_bf16(a)
    bh, bl = _split_bf16(b)
    return _dot(ah, bh) + _dot(al, bh) + _dot(ah, bl)


def _tile_lanes(x, n):
    return jnp.concatenate([x] * n, axis=1)


def _rows(i, size):
    if isinstance(i, int):
        return pl.ds(i * size, size)
    return pl.ds(pl.multiple_of(i * size, size), size)


def _software_pipeline(n_items, stages, order, rounds_per_trip, static_items=0):
    depth = len(stages)
    n_rounds = n_items + depth - 1
    assert sorted(order) == list(range(depth)) and rounds_per_trip % 2 == 0

    def static_round(r):
        for k in order:
            if 0 <= r - k < n_items:
                stages[k](r - k, (r - k) % 2)

    n_steady = max(n_items - static_items - depth + 1, 0)
    n_trips = n_steady // rounds_per_trip
    first = n_items - n_trips * rounds_per_trip if n_trips else n_rounds
    for r in range(first):
        static_round(r)
    if n_trips:
        def trip(i, carry):
            for dr in range(rounds_per_trip):
                for k in order:
                    stages[k](first + rounds_per_trip * i + dr - k, (first + dr - k) % 2)
            return carry

        lax.fori_loop(0, n_trips, trip, 0)
    for r in range(n_items if n_trips else n_rounds, n_rounds):
        static_round(r)


def _norm_matmul_body(*refs, col_tile, scaled_cols, scale, lowrank):
    if lowrank:
        x_ref, g_ref, w_ref, wl_ref, o_ref, lr_ref = refs
    else:
        x_ref, g_ref, w_ref, o_ref = refs
    xn = _rms(x_ref[...], g_ref[...])
    xh = xn.astype(BF16)
    if lowrank:
        xl = (xn - xh.astype(F32)).astype(BF16)
        both = _dot(xh, wl_ref[...])
        lr_ref[...] = both[:, :LANES] + both[:, LANES:] + _dot(xl, wl_ref[:, :LANES])
    for c0 in range(0, o_ref.shape[1], col_tile):
        acc = _dot(xh, w_ref[:, c0:c0 + col_tile])
        if c0 < scaled_cols:
            acc = acc * scale
        o_ref[:, c0:c0 + col_tile] = acc.astype(o_ref.dtype)


def _norm_matmul(x, g, w, *, scaled_cols=0, scale=1.0, w_lowrank=None):
    t, d = x.shape
    n = w.shape[1]
    tm = min(ROW_TILE, t)
    tn = min(COL_TILE, n)
    assert t % tm == 0 and n % tn == 0 and scaled_cols % tn == 0
    lowrank = w_lowrank is not None

    def resident(shape):
        return pl.BlockSpec(shape, lambda i: (0, 0))

    in_specs = [pl.BlockSpec((tm, d), lambda i: (i, 0)), resident((1, d)), resident((d, n))]
    args = [x, g.reshape(1, d), w]
    out_shape = [jax.ShapeDtypeStruct((t, n), BF16)]
    out_specs = [pl.BlockSpec((tm, n), lambda i: (i, 0))]
    if lowrank:
        in_specs.append(resident((d, 2 * LANES)))
        args.append(w_lowrank)
        out_shape.append(jax.ShapeDtypeStruct((t, LANES), F32))
        out_specs.append(pl.BlockSpec((tm, LANES), lambda i: (i, 0)))
    out = pl.pallas_call(
        functools.partial(_norm_matmul_body, col_tile=tn, scaled_cols=scaled_cols, scale=scale, lowrank=lowrank),
        grid=(t // tm,),
        in_specs=in_specs,
        out_specs=out_specs,
        out_shape=out_shape,
        compiler_params=pltpu.CompilerParams(dimension_semantics=("parallel",), vmem_limit_bytes=VMEM_LIMIT),
        name="norm_matmul",
    )(*args)
    return out if lowrank else out[0]


def _log_sigmoid(x):
    return jnp.minimum(x, 0.0) - jnp.log(1.0 + jnp.exp(-jnp.abs(x)))


def _gla_tables(c):
    n_lev = c.bit_length() - 1
    assert 1 << n_lev == c
    i = np.arange(c)[:, None]
    t = np.arange(c)[None, :]
    sums = [t <= i, t > i]
    pairs = []
    for lev in range(n_lev):
        s = c >> (lev + 1)
        blk = i // s
        odd = blk % 2 == 1
        sums.append(np.where(odd, (t >= blk * s) & (t <= i), (t > i) & (t < (blk + 1) * s)))
        pairs.append(odd & (t // s == blk - 1))
    pairs.append(i == t)
    pairs = np.stack(pairs).astype(np.float32)
    assert (pairs.sum(0) == np.tril(np.ones((c, c)))).all()
    return np.concatenate(sums).astype(np.float32), pairs


def _gla_body(q_ref, k_ref, v_ref, gate_ref, glr_ref, w2_ref, bias_ref, gn_ref, sums_ref, pairs_ref, o_ref,
              st_ref, stb_ref, *stage_refs, chunk):
    la_refs, e_refs, ql_refs, kl_refs, ab_refs, qb_refs, kd_refs, dec_refs = (
        stage_refs[2 * i:2 * i + 2] for i in range(8))
    c = chunk
    n_lev = pairs_ref.shape[0] - 1
    n_chunks = q_ref.shape[0] // c
    rank = GLA_GATE_RANK

    @pl.when(pl.program_id(2) == 0)
    def _():
        st_ref[...] = jnp.zeros_like(st_ref)
        stb_ref[...] = jnp.zeros_like(stb_ref)

    lane = lax.broadcasted_iota(jnp.int32, (c, LANES), 1)
    lo_lanes = lane // rank == 1

    def gates(i, slot):
        glr = glr_ref[_rows(i, c), :]
        hi = glr.astype(BF16)
        lo = (glr - hi.astype(F32)).astype(BF16)
        gk = _dot(jnp.where(lo_lanes, lo, hi), w2_ref[...]) + bias_ref[...]
        la = _log_sigmoid(gk) * (LOG2E / GLA_GATE_NORMALIZER)
        la_hi, la_lo = _split_bf16(la)
        la_refs[slot][0:c, :] = la_hi
        la_refs[slot][c:2 * c, :] = la_lo

    def decays(i, slot):
        rows = _rows(i, c)
        e = _dot(sums_ref[...], la_refs[slot][...])
        e_refs[slot][...] = e[0:2 * c]
        qf = q_ref[rows, :].astype(F32)
        kf = k_ref[rows, :].astype(F32)
        for lev in range(n_lev):
            w = jnp.exp2(e[(lev + 2) * c:(lev + 3) * c])
            ql_refs[slot][lev] = (qf * w).astype(BF16)
            kl_refs[slot][lev] = (kf * w).astype(BF16)

    def scores(i, slot):
        rows = _rows(i, c)
        q = q_ref[rows, :]
        k = k_ref[rows, :]
        a = pairs_ref[n_lev] * _dot_nt(q, k)
        for lev in range(n_lev):
            a = a + pairs_ref[lev] * _dot_nt(ql_refs[slot][lev], kl_refs[slot][lev])
        ab_refs[slot][...] = a.astype(BF16)
        e = e_refs[slot][...]
        b = e[0:c]
        qb_refs[slot][...] = (q.astype(F32) * jnp.exp2(b)).astype(BF16)
        kd_refs[slot][...] = (k.astype(F32) * jnp.exp2(e[c:2 * c])).astype(BF16)
        dec_refs[slot][...] = jnp.broadcast_to(jnp.exp2(b[c - 1:c, :]), dec_refs[slot].shape)

    def output(i, slot):
        rows = _rows(i, c)
        v = v_ref[rows, :]
        o = _dot_nt(qb_refs[slot][...], stb_ref[...]) + _dot(ab_refs[slot][...], v)
        st = st_ref[...] * dec_refs[slot][0:1, :] + _dot_tn(v, kd_refs[slot][...])
        st_ref[...] = st
        stb_ref[...] = st.astype(BF16)
        g = gate_ref[rows, :].astype(F32)
        o_ref[rows, :] = (_rms(o, gn_ref[...]) * (g * jax.nn.sigmoid(g))).astype(o_ref.dtype)

    _software_pipeline(n_chunks, (gates, decays, scores, output), GLA_STAGE_ORDER, GLA_ROUNDS_PER_TRIP)


def _gla_mixer(proj, glr, w_gk2, b_gk, g_norm, *, batch, seq):
    t = proj.shape[0]
    key_w = w_gk2.shape[1]
    dk = key_w // GLA_HEADS
    mix_w = 2 * key_w
    dv = mix_w // GLA_HEADS
    rows = min(GLA_ROWS, seq)
    nblk = seq // rows
    c = min(GLA_CHUNK, rows)
    assert seq % rows == 0 and rows % c == 0
    k_off = key_w // dk
    v_off = 2 * key_w // dv
    g_off = (2 * key_w + mix_w) // dv
    w_hi, w_lo = _split_bf16(w_gk2)
    w2 = jnp.zeros((LANES, key_w), BF16).at[:3 * GLA_GATE_RANK].set(jnp.concatenate([w_hi, w_hi, w_lo]))
    sums, pairs = _gla_tables(c)
    n_lev = pairs.shape[0] - 1
    sums = jnp.asarray(np.concatenate([sums, sums], axis=1), BF16)
    pairs = jnp.asarray(pairs, F32)

    def rowmap(off):
        return lambda b, h, n: (b * nblk + n, off + h)

    def two(shape, dtype):
        return [pltpu.VMEM(shape, dtype), pltpu.VMEM(shape, dtype)]

    return pl.pallas_call(
        functools.partial(_gla_body, chunk=c),
        grid=(batch, GLA_HEADS, nblk),
        in_specs=[
            pl.BlockSpec((rows, dk), rowmap(0)),
            pl.BlockSpec((rows, dk), rowmap(k_off)),
            pl.BlockSpec((rows, dv), rowmap(v_off)),
            pl.BlockSpec((rows, dv), rowmap(g_off)),
            pl.BlockSpec((rows, LANES), lambda b, h, n: (b * nblk + n, 0)),
            pl.BlockSpec((LANES, dk), lambda b, h, n: (0, h)),
            pl.BlockSpec((1, dk), lambda b, h, n: (0, h)),
            pl.BlockSpec((1, dv), lambda b, h, n: (0, 0)),
            pl.BlockSpec(sums.shape, lambda b, h, n: (0, 0)),
            pl.BlockSpec(pairs.shape, lambda b, h, n: (0, 0, 0)),
        ],
        out_specs=pl.BlockSpec((rows, dv), rowmap(0)),
        out_shape=jax.ShapeDtypeStruct((t, mix_w), BF16),
        scratch_shapes=(
            [pltpu.VMEM((dv, dk), F32), pltpu.VMEM((dv, dk), BF16)]
            + two((2 * c, dk), BF16)
            + two((2 * c, dk), F32)
            + two((n_lev, c, dk), BF16) + two((n_lev, c, dk), BF16)
            + two((c, c), BF16)
            + two((c, dk), BF16) + two((c, dk), BF16)
            + two((8, dk), F32)),
        compiler_params=pltpu.CompilerParams(
            dimension_semantics=("parallel", "parallel", "arbitrary"), vmem_limit_bytes=VMEM_LIMIT),
        name="gla_mixer",
    )(proj, proj, proj, proj, glr, w2, b_gk.reshape(1, key_w), g_norm.reshape(1, dv), sums, pairs)


def _diff_attn_body(qt_ref, kt_ref, q_ref, k_ref, v_ref, gate_ref, lam_ref, gn_ref, o_ref, s0_ref, s1_ref,
                    mx0_ref, mx1_ref, p0_ref, p1_ref, al0_ref, al1_ref, m_ref, l_ref, acc_ref,
                    *, tile, tiles, lambda_init):
    d = DIFF_HEAD_DIM
    n_tiles = q_ref.shape[0] // tile
    s_refs, mx_refs = (s0_ref, s1_ref), (mx0_ref, mx1_ref)
    p_refs, al_refs = (p0_ref, p1_ref), (al0_ref, al1_ref)
    reps = tile // LANES
    wide = 2 * d // LANES

    def item(n):
        return tiles[n] if isinstance(n, int) else (qt_ref[n], kt_ref[n])

    def is_diagonal(n):
        return isinstance(n, int) and n < n_tiles

    half = tile // 2
    bands = ((0, half, half), (half, tile, tile))

    def scores(n, slot):
        qt, kt = item(n)
        for t in range(2):
            cols = slice(t * d, (t + 1) * d)
            if is_diagonal(n):
                for r0, r1, nk in bands:
                    s_refs[slot][t, r0:r1, 0:nk] = _dot_nt(q_ref[pl.ds(qt * tile + r0, r1 - r0), cols],
                                                           k_ref[pl.ds(kt * tile, nk), cols])
            else:
                s = _dot_nt(q_ref[_rows(qt, tile), cols], k_ref[_rows(kt, tile), cols])
                s_refs[slot][t] = s
                mx_refs[slot][t] = jnp.broadcast_to(jnp.max(s, axis=1, keepdims=True), (tile, LANES))

    def weights(n, slot):
        qt = item(n)[0]
        for t in range(2):
            if is_diagonal(n):
                for r0, r1, nk in bands:
                    q_rows = pl.ds(qt * tile + r0, r1 - r0)
                    s = s_refs[slot][t, r0:r1, 0:nk]
                    row = lax.broadcasted_iota(jnp.int32, s.shape, 0) + r0
                    col = lax.broadcasted_iota(jnp.int32, s.shape, 1)
                    s = jnp.where(col <= row, s, -1e30)
                    m_new = jnp.broadcast_to(jnp.max(s, axis=1, keepdims=True), (r1 - r0, LANES))
                    p = jnp.exp2(s - _tile_lanes(m_new, nk // LANES))
                    l_ref[t, q_rows, :] = jnp.broadcast_to(jnp.sum(p, axis=1, keepdims=True), (r1 - r0, LANES))
                    m_ref[t, q_rows, :] = m_new
                    p_refs[slot][t, r0:r1, 0:nk] = p.astype(BF16)
            else:
                q_rows = _rows(qt, tile)
                m_old = m_ref[t, q_rows, :]
                m_new = jnp.maximum(m_old, mx_refs[slot][t])
                alpha = jnp.exp2(m_old - m_new)
                p = jnp.exp2(s_refs[slot][t] - _tile_lanes(m_new, reps))
                l_ref[t, q_rows, :] = alpha * l_ref[t, q_rows, :] + jnp.sum(p, axis=1, keepdims=True)
                m_ref[t, q_rows, :] = m_new
                al_refs[slot][t] = alpha
                p_refs[slot][t] = p.astype(BF16)

    def values(n, slot):
        qt, kt = item(n)
        for t in range(2):
            if is_diagonal(n):
                for r0, r1, nk in bands:
                    acc_ref[t, pl.ds(qt * tile + r0, r1 - r0), :] = _dot(p_refs[slot][t, r0:r1, 0:nk],
                                                                        v_ref[pl.ds(kt * tile, nk), :])
            else:
                q_rows = _rows(qt, tile)
                acc_ref[t, q_rows, :] = (acc_ref[t, q_rows, :] * _tile_lanes(al_refs[slot][t], wide)
                                         + _dot(p_refs[slot][t], v_ref[_rows(kt, tile), :]))

    _software_pipeline(len(tiles), (scores, weights, values), ATTN_STAGE_ORDER, ATTN_ROUNDS_PER_TRIP,
                       static_items=n_tiles)

    lam = lam_ref[...]
    lam_full = (jnp.exp(jnp.sum(lam[0:1] * lam[1:2], axis=1, keepdims=True))
                - jnp.exp(jnp.sum(lam[2:3] * lam[3:4], axis=1, keepdims=True)) + lambda_init)

    gain = gn_ref[...] * (1.0 - lambda_init)

    def finish(qt, carry):
        rows = _rows(qt, tile)
        o1 = acc_ref[0, rows, :] * _tile_lanes(1.0 / l_ref[0, rows, :], wide)
        o2 = acc_ref[1, rows, :] * _tile_lanes(lam_full / l_ref[1, rows, :], wide)
        o = _rms(o1 - o2, gain)
        g = gate_ref[rows, :].astype(F32)
        o_ref[rows, :] = (o * (g * jax.nn.sigmoid(g))).astype(o_ref.dtype)
        return carry

    lax.fori_loop(0, n_tiles, finish, 0)


def _diff_mixer(qg, kv, lam, g_norm, lambda_init, *, batch, seq):
    t = qg.shape[0]
    mix_w = qg.shape[1] // 2
    hw = 2 * DIFF_HEAD_DIM
    heads = mix_w // hw
    tile = min(ATTN_TILE, seq)
    nq = seq // tile
    assert seq % tile == 0
    tiles = tuple((i, i) for i in range(nq)) + tuple((i, j) for i in range(nq) for j in range(i))
    qt_tab = jnp.asarray([qt for qt, _ in tiles], jnp.int32)
    kt_tab = jnp.asarray([kt for _, kt in tiles], jnp.int32)

    def two(shape, dtype):
        return [pltpu.VMEM(shape, dtype), pltpu.VMEM(shape, dtype)]

    return pl.pallas_call(
        functools.partial(_diff_attn_body, tile=tile, tiles=tiles, lambda_init=lambda_init),
        grid_spec=pltpu.PrefetchScalarGridSpec(
            num_scalar_prefetch=2,
            grid=(batch, heads),
            in_specs=[
                pl.BlockSpec((seq, hw), lambda b, h, *_: (b, h)),
                pl.BlockSpec((seq, hw), lambda b, h, *_: (b, h)),
                pl.BlockSpec((seq, hw), lambda b, h, *_: (b, heads + h)),
                pl.BlockSpec((seq, hw), lambda b, h, *_: (b, heads + h)),
                pl.BlockSpec((4, DIFF_HEAD_DIM), lambda b, h, *_: (0, 0)),
                pl.BlockSpec((1, hw), lambda b, h, *_: (0, 0)),
            ],
            out_specs=pl.BlockSpec((seq, hw), lambda b, h, *_: (b, h)),
            scratch_shapes=(
                two((2, tile, tile), F32)
                + two((2, tile, LANES), F32)
                + two((2, tile, tile), BF16)
                + two((2, tile, LANES), F32)
                + [pltpu.VMEM((2, seq, LANES), F32),
                   pltpu.VMEM((2, seq, LANES), F32),
                   pltpu.VMEM((2, seq, hw), F32)]),
        ),
        out_shape=jax.ShapeDtypeStruct((t, mix_w), BF16),
        compiler_params=pltpu.CompilerParams(
            dimension_semantics=("parallel", "parallel"), vmem_limit_bytes=VMEM_LIMIT),
        name="diff_attn",
    )(qt_tab, kt_tab, qg, kv, kv, qg, lam, g_norm.reshape(1, hw))


def _out_ple_body(o_ref, h_ref, p_ref, wo_ref, pn_ref, wg_ref, wp_ref, fn_ref, out_ref, *, final):
    hm = h_ref[...] + _dot(o_ref[...], wo_ref[...])
    hn = _rms(hm, pn_ref[...]).astype(BF16)
    gate = jax.nn.sigmoid(_dot(hn, wg_ref[...]))
    emb = _dot(p_ref[...].astype(BF16), wp_ref[...])
    h2 = hm + gate * emb
    if final:
        h2 = _rms(h2, fn_ref[...])
    out_ref[...] = h2


def _out_ple(o, h, p, layer, w_out, ple_norm, w_gate, w_proj, final_norm, *, final):
    t, d = h.shape
    mix_w = o.shape[1]
    pd = p.shape[2]
    tm = min(OUT_ROW_TILE, t)
    assert t % tm == 0

    def const(shape):
        return pl.BlockSpec(shape, lambda i: (0, 0))

    return pl.pallas_call(
        functools.partial(_out_ple_body, final=final),
        grid=(t // tm,),
        in_specs=[
            pl.BlockSpec((tm, mix_w), lambda i: (i, 0)),
            pl.BlockSpec((tm, d), lambda i: (i, 0)),
            pl.BlockSpec((None, tm, pd), lambda i: (layer, i, 0)),
            const((mix_w, d)),
            const((1, d)),
            const((d, d)),
            const((pd, d)),
            const((1, d)),
        ],
        out_specs=pl.BlockSpec((tm, d), lambda i: (i, 0)),
        out_shape=jax.ShapeDtypeStruct((t, d), F32),
        compiler_params=pltpu.CompilerParams(
            dimension_semantics=("parallel",), vmem_limit_bytes=VMEM_LIMIT),
        name="out_ple",
    )(o, h, p, w_out, ple_norm.reshape(1, d), w_gate, w_proj, final_norm.reshape(1, d))


def kernel(x, p, norm_mix, gla_w_in, gla_w_gk2, gla_b_gk, gla_norm, gla_w_out, kv_norm, w_kv, diff_w_in,
           diff_lambda, diff_norm, diff_w_out, ple_norm, ple_w_gate, ple_w_proj, final_norm):
    batch, seq, d = x.shape
    depth = p.shape[0]
    n_gla = gla_w_in.shape[0]
    t = batch * seq
    key_w = gla_w_gk2.shape[2]
    main_cols = gla_w_in.shape[2] - GLA_GATE_RANK

    h = x.reshape(t, d)
    p = p.reshape(depth, t, p.shape[3])
    kv = None
    for i in range(depth):
        if i < n_gla:
            w_main = gla_w_in[i, :, :main_cols].astype(BF16)
            w_gate = gla_w_in[i, :, main_cols:]
            w_lr = jnp.zeros((d, LANES), F32).at[:, :3 * GLA_GATE_RANK].set(jnp.concatenate([w_gate] * 3, axis=1))
            w_lr = jnp.concatenate(_split_bf16(w_lr), axis=1)
            proj, glr = _norm_matmul(h, norm_mix[i], w_main, scaled_cols=key_w,
                                     scale=(key_w // GLA_HEADS) ** -0.5, w_lowrank=w_lr)
            o = _gla_mixer(proj, glr, gla_w_gk2[i], gla_b_gk[i], gla_norm[i], batch=batch, seq=seq)
            w_out = gla_w_out[i]
        else:
            j = i - n_gla
            lambda_init = 0.8 - 0.6 * math.exp(-0.3 * i)
            qg = _norm_matmul(h, norm_mix[i], diff_w_in[j].astype(BF16), scaled_cols=diff_w_in.shape[2] // 2,
                              scale=DIFF_HEAD_DIM ** -0.5 * LOG2E)
            o = _diff_mixer(qg, kv, diff_lambda[j], diff_norm[j], lambda_init, batch=batch, seq=seq)
            w_out = diff_w_out[j]
        h = _out_ple(o, h, p, i, w_out.astype(BF16), ple_norm[i], ple_w_gate[i].astype(BF16),
                     ple_w_proj[i].astype(BF16), final_norm, final=(i == depth - 1))
        if i == n_gla - 1:
            kv = _norm_matmul(h, kv_norm, w_kv.astype(BF16))
    return h.reshape(batch, seq, d)
```

```python
import functools
import math

import jax
import jax.numpy as jnp
import numpy as np
from jax import lax
from jax.experimental import pallas as pl
from jax.experimental.pallas import tpu as pltpu

F32 = jnp.float32
BF16 = jnp.bfloat16
EPS = 1e-6
LOG2E = math.log2(math.e)

EXPAND = 2
PLE_DIM = 256
GLA_HEADS = 4
GLA_GATE_RANK = 16
GLA_GATE_NORMALIZER = 16.0
DIFF_HEAD_DIM = 128

LANES = 128
VMEM_LIMIT = 56 * 1024 * 1024

GLA_CHUNK = 128
GLA_ROWS = 4096
GLA_STAGE_ORDER = (0, 3, 1, 2)
GLA_ROUNDS_PER_TRIP = 4
ATTN_TILE = 512
ATTN_STAGE_ORDER = (2, 1, 0)
ATTN_ROUNDS_PER_TRIP = 4
ROW_TILE = 1024
OUT_ROW_TILE = 1024
COL_TILE = 1024


def _rms(x, g):
    return x * lax.rsqrt(jnp.mean(x * x, axis=-1, keepdims=True) + EPS) * g


def _split_bf16(x):
    hi = x.astype(BF16)
    lo = (x - hi.astype(F32)).astype(BF16)
    return hi, lo


def _dot(a, b):
    return jnp.dot(a, b, preferred_element_type=F32)


def _dot_nt(a, b):
    return lax.dot_general(a, b, (((1,), (1,)), ((), ())), preferred_element_type=F32)


def _dot_tn(a, b):
    return lax.dot_general(a, b, (((0,), (0,)), ((), ())), preferred_element_type=F32)


def _dot_split(a, b):
    ah, al = _split_bf16(a)
    bh, bl = _split_bf16(b)
    return _dot(ah, bh) + _dot(al, bh) + _dot(ah, bl)


def _tile_lanes(x, n):
    return jnp.concatenate([x] * n, axis=1)


def _rows(i, size):
    if isinstance(i, int):
        return pl.ds(i * size, size)
    return pl.ds(pl.multiple_of(i * size, size), size)


def _software_pipeline(n_items, stages, order, rounds_per_trip, static_items=0):
    depth = len(stages)
    n_rounds = n_items + depth - 1
    assert sorted(order) == list(range(depth)) and rounds_per_trip % 2 == 0

    def static_round(r):
        for k in order:
            if 0 <= r - k < n_items:
                stages[k](r - k, (r - k) % 2)

    n_steady = max(n_items - static_items - depth + 1, 0)
    n_trips = n_steady // rounds_per_trip
    first = n_items - n_trips * rounds_per_trip if n_trips else n_rounds
    for r in range(first):
        static_round(r)
    if n_trips:
        def trip(i, carry):
            for dr in range(rounds_per_trip):
                for k in order:
                    stages[k](first + rounds_per_trip * i + dr - k, (first + dr - k) % 2)
            return carry

        lax.fori_loop(0, n_trips, trip, 0)
    for r in range(n_items if n_trips else n_rounds, n_rounds):
        static_round(r)


def _norm_matmul_body(*refs, col_tile, scaled_cols, scale, lowrank):
    if lowrank:
        x_ref, g_ref, w_ref, wl_ref, o_ref, lr_ref = refs
    else:
        x_ref, g_ref, w_ref, o_ref = refs
    tm = x_ref.shape[0]
    for r in (slice(0, tm // 2), slice(tm // 2, tm)):
        xn = _rms(x_ref[r, :], g_ref[...])
        xh = xn.astype(BF16)
        if lowrank:
            xl = (xn - xh.astype(F32)).astype(BF16)
            both = _dot(xh, wl_ref[...])
            lr_ref[r, :] = both[:, :LANES] + both[:, LANES:] + _dot(xl, wl_ref[:, :LANES])
        for c0 in range(0, o_ref.shape[1], col_tile):
            acc = _dot(xh, w_ref[:, c0:c0 + col_tile])
            if c0 < scaled_cols:
                acc = acc * scale
            o_ref[r, c0:c0 + col_tile] = acc.astype(o_ref.dtype)


def _norm_matmul(x, g, w, *, scaled_cols=0, scale=1.0, w_lowrank=None):
    t, d = x.shape
    n = w.shape[1]
    tm = min(ROW_TILE, t)
    tn = min(COL_TILE, n)
    assert t % tm == 0 and n % tn == 0 and scaled_cols % tn == 0
    lowrank = w_lowrank is not None

    def resident(shape):
        return pl.BlockSpec(shape, lambda i: (0, 0), pipeline_mode=pl.Buffered(1))

    in_specs = [pl.BlockSpec((tm, d), lambda i: (i, 0)), resident((1, d)), resident((d, n))]
    args = [x, g.reshape(1, d), w]
    out_shape = [jax.ShapeDtypeStruct((t, n), BF16)]
    out_specs = [pl.BlockSpec((tm, n), lambda i: (i, 0))]
    if lowrank:
        in_specs.append(resident((d, 2 * LANES)))
        args.append(w_lowrank)
        out_shape.append(jax.ShapeDtypeStruct((t, LANES), F32))
        out_specs.append(pl.BlockSpec((tm, LANES), lambda i: (i, 0)))
    out = pl.pallas_call(
        functools.partial(_norm_matmul_body, col_tile=tn, scaled_cols=scaled_cols, scale=scale, lowrank=lowrank),
        grid=(t // tm,),
        in_specs=in_specs,
        out_specs=out_specs,
        out_shape=out_shape,
        compiler_params=pltpu.CompilerParams(dimension_semantics=("parallel",), vmem_limit_bytes=VMEM_LIMIT),
        name="norm_matmul",
    )(*args)
    return out if lowrank else out[0]


def _log_sigmoid(x):
    return jnp.minimum(x, 0.0) - jnp.log(1.0 + jnp.exp(-jnp.abs(x)))


def _gla_tables(c):
    n_lev = c.bit_length() - 1
    assert 1 << n_lev == c
    i = np.arange(c)[:, None]
    t = np.arange(c)[None, :]
    sums = [t <= i, t > i]
    pairs = []
    for lev in range(n_lev):
        s = c >> (lev + 1)
        blk = i // s
        odd = blk % 2 == 1
        sums.append(np.where(odd, (t >= blk * s) & (t <= i), (t > i) & (t < (blk + 1) * s)))
        pairs.append(odd & (t // s == blk - 1))
    pairs.append(i == t)
    pairs = np.stack(pairs).astype(np.float32)
    assert (pairs.sum(0) == np.tril(np.ones((c, c)))).all()
    return np.concatenate(sums).astype(np.float32), pairs


def _gla_body(q_ref, k_ref, v_ref, gate_ref, glr_ref, w2_ref, bias_ref, gn_ref, sums_ref, pairs_ref, o_ref,
              st_ref, stb_ref, *stage_refs, chunk):
    la_refs, e_refs, ql_refs, kl_refs, ab_refs, qb_refs, kd_refs, dec_refs = (
        stage_refs[2 * i:2 * i + 2] for i in range(8))
    c = chunk
    n_lev = pairs_ref.shape[0] - 1
    n_chunks = q_ref.shape[0] // c
    rank = GLA_GATE_RANK

    @pl.when(pl.program_id(2) == 0)
    def _():
        st_ref[...] = jnp.zeros_like(st_ref)
        stb_ref[...] = jnp.zeros_like(stb_ref)

    lane = lax.broadcasted_iota(jnp.int32, (c, LANES), 1)
    lo_lanes = lane // rank == 1

    def gates(i, slot):
        glr = glr_ref[_rows(i, c), :]
        hi = glr.astype(BF16)
        lo = (glr - hi.astype(F32)).astype(BF16)
        gk = _dot(jnp.where(lo_lanes, lo, hi), w2_ref[...]) + bias_ref[...]
        la = _log_sigmoid(gk) * (LOG2E / GLA_GATE_NORMALIZER)
        la_hi, la_lo = _split_bf16(la)
        la_refs[slot][0:c, :] = la_hi
        la_refs[slot][c:2 * c, :] = la_lo

    def decays(i, slot):
        rows = _rows(i, c)
        e = _dot(sums_ref[...], la_refs[slot][...])
        e_refs[slot][...] = e[0:2 * c]
        qf = q_ref[rows, :].astype(F32)
        kf = k_ref[rows, :].astype(F32)
        for lev in range(n_lev):
            w = jnp.exp2(e[(lev + 2) * c:(lev + 3) * c])
            ql_refs[slot][lev] = (qf * w).astype(BF16)
            kl_refs[slot][lev] = (kf * w).astype(BF16)

    def scores(i, slot):
        rows = _rows(i, c)
        q = q_ref[rows, :]
        k = k_ref[rows, :]
        a = pairs_ref[n_lev] * _dot_nt(q, k)
        for lev in range(n_lev):
            a = a + pairs_ref[lev] * _dot_nt(ql_refs[slot][lev], kl_refs[slot][lev])
        ab_refs[slot][...] = a.astype(BF16)
        e = e_refs[slot][...]
        b = e[0:c]
        qb_refs[slot][...] = (q.astype(F32) * jnp.exp2(b)).astype(BF16)
        kd_refs[slot][...] = (k.astype(F32) * jnp.exp2(e[c:2 * c])).astype(BF16)
        dec_refs[slot][...] = jnp.broadcast_to(jnp.exp2(b[c - 1:c, :]), dec_refs[slot].shape)

    def output(i, slot):
        rows = _rows(i, c)
        v = v_ref[rows, :]
        o = _dot_nt(qb_refs[slot][...], stb_ref[...]) + _dot(ab_refs[slot][...], v)
        st = st_ref[...] * dec_refs[slot][0:1, :] + _dot_tn(v, kd_refs[slot][...])
        st_ref[...] = st
        stb_ref[...] = st.astype(BF16)
        g = gate_ref[rows, :].astype(F32)
        o_ref[rows, :] = (_rms(o, gn_ref[...]) * (g * jax.nn.sigmoid(g))).astype(o_ref.dtype)

    _software_pipeline(n_chunks, (gates, decays, scores, output), GLA_STAGE_ORDER, GLA_ROUNDS_PER_TRIP)


def _gla_mixer(proj, glr, w_gk2, b_gk, g_norm, *, batch, seq):
    t = proj.shape[0]
    key_w = w_gk2.shape[1]
    dk = key_w // GLA_HEADS
    mix_w = 2 * key_w
    dv = mix_w // GLA_HEADS
    rows = min(GLA_ROWS, seq)
    nblk = seq // rows
    c = min(GLA_CHUNK, rows)
    assert seq % rows == 0 and rows % c == 0
    k_off = key_w // dk
    v_off = 2 * key_w // dv
    g_off = (2 * key_w + mix_w) // dv
    w_hi, w_lo = _split_bf16(w_gk2)
    w2 = jnp.zeros((LANES, key_w), BF16).at[:3 * GLA_GATE_RANK].set(jnp.concatenate([w_hi, w_hi, w_lo]))
    sums, pairs = _gla_tables(c)
    n_lev = pairs.shape[0] - 1
    sums = jnp.asarray(np.concatenate([sums, sums], axis=1), BF16)
    pairs = jnp.asarray(pairs, F32)

    def rowmap(off):
        return lambda b, h, n: (b * nblk + n, off + h)

    def two(shape, dtype):
        return [pltpu.VMEM(shape, dtype), pltpu.VMEM(shape, dtype)]

    return pl.pallas_call(
        functools.partial(_gla_body, chunk=c),
        grid=(batch, GLA_HEADS, nblk),
        in_specs=[
            pl.BlockSpec((rows, dk), rowmap(0)),
            pl.BlockSpec((rows, dk), rowmap(k_off)),
            pl.BlockSpec((rows, dv), rowmap(v_off)),
            pl.BlockSpec((rows, dv), rowmap(g_off)),
            pl.BlockSpec((rows, LANES), lambda b, h, n: (b * nblk + n, 0)),
            pl.BlockSpec((LANES, dk), lambda b, h, n: (0, h)),
            pl.BlockSpec((1, dk), lambda b, h, n: (0, h)),
            pl.BlockSpec((1, dv), lambda b, h, n: (0, 0)),
            pl.BlockSpec(sums.shape, lambda b, h, n: (0, 0)),
            pl.BlockSpec(pairs.shape, lambda b, h, n: (0, 0, 0)),
        ],
        out_specs=pl.BlockSpec((rows, dv), rowmap(0)),
        out_shape=jax.ShapeDtypeStruct((t, mix_w), BF16),
        scratch_shapes=(
            [pltpu.VMEM((dv, dk), F32), pltpu.VMEM((dv, dk), BF16)]
            + two((2 * c, dk), BF16)
            + two((2 * c, dk), F32)
            + two((n_lev, c, dk), BF16) + two((n_lev, c, dk), BF16)
            + two((c, c), BF16)
            + two((c, dk), BF16) + two((c, dk), BF16)
            + two((8, dk), F32)),
        compiler_params=pltpu.CompilerParams(
            dimension_semantics=("parallel", "parallel", "arbitrary"), vmem_limit_bytes=VMEM_LIMIT),
        name="gla_mixer",
    )(proj, proj, proj, proj, glr, w2, b_gk.reshape(1, key_w), g_norm.reshape(1, dv), sums, pairs)


def _diff_attn_body(qt_ref, kt_ref, q_ref, k_ref, v_ref, gate_ref, lam_ref, gn_ref, o_ref, s0_ref, s1_ref,
                    mx0_ref, mx1_ref, p0_ref, p1_ref, al0_ref, al1_ref, m_ref, l_ref, acc_ref,
                    *, tile, tiles, lambda_init):
    d = DIFF_HEAD_DIM
    n_tiles = q_ref.shape[0] // tile
    s_refs, mx_refs = (s0_ref, s1_ref), (mx0_ref, mx1_ref)
    p_refs, al_refs = (p0_ref, p1_ref), (al0_ref, al1_ref)
    reps = tile // LANES
    wide = 2 * d // LANES

    def item(n):
        return tiles[n] if isinstance(n, int) else (qt_ref[n], kt_ref[n])

    def is_diagonal(n):
        return isinstance(n, int) and n < n_tiles

    half = tile // 2
    bands = ((0, half, half), (half, tile, tile))

    def scores(n, slot):
        qt, kt = item(n)
        for t in range(2):
            cols = slice(t * d, (t + 1) * d)
            if is_diagonal(n):
                for r0, r1, nk in bands:
                    s_refs[slot][t, r0:r1, 0:nk] = _dot_nt(q_ref[pl.ds(qt * tile + r0, r1 - r0), cols],
                                                           k_ref[pl.ds(kt * tile, nk), cols])
            else:
                s = _dot_nt(q_ref[_rows(qt, tile), cols], k_ref[_rows(kt, tile), cols])
                s_refs[slot][t] = s
                mx_refs[slot][t] = jnp.broadcast_to(jnp.max(s, axis=1, keepdims=True), (tile, LANES))

    def weights(n, slot):
        qt = item(n)[0]
        for t in range(2):
            if is_diagonal(n):
                for r0, r1, nk in bands:
                    q_rows = pl.ds(qt * tile + r0, r1 - r0)
                    s = s_refs[slot][t, r0:r1, 0:nk]
                    row = lax.broadcasted_iota(jnp.int32, s.shape, 0) + r0
                    col = lax.broadcasted_iota(jnp.int32, s.shape, 1)
                    s = jnp.where(col <= row, s, -1e30)
                    m_new = jnp.broadcast_to(jnp.max(s, axis=1, keepdims=True), (r1 - r0, LANES))
                    p = jnp.exp2(s - _tile_lanes(m_new, nk // LANES))
                    l_ref[t, q_rows, :] = jnp.broadcast_to(jnp.sum(p, axis=1, keepdims=True), (r1 - r0, LANES))
                    m_ref[t, q_rows, :] = m_new
                    p_refs[slot][t, r0:r1, 0:nk] = p.astype(BF16)
            else:
                q_rows = _rows(qt, tile)
                m_old = m_ref[t, q_rows, :]
                m_new = jnp.maximum(m_old, mx_refs[slot][t])
                alpha = jnp.exp2(m_old - m_new)
                p = jnp.exp2(s_refs[slot][t] - _tile_lanes(m_new, reps))
                l_ref[t, q_rows, :] = alpha * l_ref[t, q_rows, :] + jnp.sum(p, axis=1, keepdims=True)
                m_ref[t, q_rows, :] = m_new
                al_refs[slot][t] = alpha
                p_refs[slot][t] = p.astype(BF16)

    def values(n, slot):
        qt, kt = item(n)
        for t in range(2):
            if is_diagonal(n):
                for r0, r1, nk in bands:
                    acc_ref[t, pl.ds(qt * tile + r0, r1 - r0), :] = _dot(p_refs[slot][t, r0:r1, 0:nk],
                                                                        v_ref[pl.ds(kt * tile, nk), :])
            else:
                q_rows = _rows(qt, tile)
                acc_ref[t, q_rows, :] = (acc_ref[t, q_rows, :] * _tile_lanes(al_refs[slot][t], wide)
                                         + _dot(p_refs[slot][t], v_ref[_rows(kt, tile), :]))

    _software_pipeline(len(tiles), (scores, weights, values), ATTN_STAGE_ORDER, ATTN_ROUNDS_PER_TRIP,
                       static_items=n_tiles)

    lam = lam_ref[...]
    lam_full = (jnp.exp(jnp.sum(lam[0:1] * lam[1:2], axis=1, keepdims=True))
                - jnp.exp(jnp.sum(lam[2:3] * lam[3:4], axis=1, keepdims=True)) + lambda_init)

    gain = gn_ref[...] * (1.0 - lambda_init)

    def finish(qt, carry):
        rows = _rows(qt, tile)
        o1 = acc_ref[0, rows, :] * _tile_lanes(1.0 / l_ref[0, rows, :], wide)
        o2 = acc_ref[1, rows, :] * _tile_lanes(lam_full / l_ref[1, rows, :], wide)
        o = _rms(o1 - o2, gain)
        g = gate_ref[rows, :].astype(F32)
        o_ref[rows, :] = (o * (g * jax.nn.sigmoid(g))).astype(o_ref.dtype)
        return carry

    lax.fori_loop(0, n_tiles, finish, 0)


def _diff_mixer(qg, kv, lam, g_norm, lambda_init, *, batch, seq):
    t = qg.shape[0]
    mix_w = qg.shape[1] // 2
    hw = 2 * DIFF_HEAD_DIM
    heads = mix_w // hw
    tile = min(ATTN_TILE, seq)
    nq = seq // tile
    assert seq % tile == 0
    tiles = tuple((i, i) for i in range(nq)) + tuple((i, j) for i in range(nq) for j in range(i))
    qt_tab = jnp.asarray([qt for qt, _ in tiles], jnp.int32)
    kt_tab = jnp.asarray([kt for _, kt in tiles], jnp.int32)

    def two(shape, dtype):
        return [pltpu.VMEM(shape, dtype), pltpu.VMEM(shape, dtype)]

    return pl.pallas_call(
        functools.partial(_diff_attn_body, tile=tile, tiles=tiles, lambda_init=lambda_init),
        grid_spec=pltpu.PrefetchScalarGridSpec(
            num_scalar_prefetch=2,
            grid=(batch, heads),
            in_specs=[
                pl.BlockSpec((seq, hw), lambda b, h, *_: (b, h)),
                pl.BlockSpec((seq, hw), lambda b, h, *_: (b, h)),
                pl.BlockSpec((seq, hw), lambda b, h, *_: (b, heads + h)),
                pl.BlockSpec((seq, hw), lambda b, h, *_: (b, heads + h)),
                pl.BlockSpec((4, DIFF_HEAD_DIM), lambda b, h, *_: (0, 0)),
                pl.BlockSpec((1, hw), lambda b, h, *_: (0, 0)),
            ],
            out_specs=pl.BlockSpec((seq, hw), lambda b, h, *_: (b, h)),
            scratch_shapes=(
                two((2, tile, tile), F32)
                + two((2, tile, LANES), F32)
                + two((2, tile, tile), BF16)
                + two((2, tile, LANES), F32)
                + [pltpu.VMEM((2, seq, LANES), F32),
                   pltpu.VMEM((2, seq, LANES), F32),
                   pltpu.VMEM((2, seq, hw), F32)]),
        ),
        out_shape=jax.ShapeDtypeStruct((t, mix_w), BF16),
        compiler_params=pltpu.CompilerParams(
            dimension_semantics=("parallel", "parallel"), vmem_limit_bytes=VMEM_LIMIT),
        name="diff_attn",
    )(qt_tab, kt_tab, qg, kv, kv, qg, lam, g_norm.reshape(1, hw))


def _out_ple_body(o_ref, h_ref, p_ref, wo_ref, pn_ref, wg_ref, wp_ref, fn_ref, out_ref, *, final):
    tm = h_ref.shape[0]
    halves = (slice(0, tm // 2), slice(tm // 2, tm))
    hm = [h_ref[r, :] + _dot(o_ref[r, :], wo_ref[...]) for r in halves]
    emb = [_dot(p_ref[r, :].astype(BF16), wp_ref[...]) for r in halves]
    for r, hm_r, emb_r in zip(halves, hm, emb):
        hn = _rms(hm_r, pn_ref[...]).astype(BF16)
        h2 = hm_r + jax.nn.sigmoid(_dot(hn, wg_ref[...])) * emb_r
        if final:
            h2 = _rms(h2, fn_ref[...])
        out_ref[r, :] = h2


def _out_ple(o, h, p, layer, w_out, ple_norm, w_gate, w_proj, final_norm, *, final):
    t, d = h.shape
    mix_w = o.shape[1]
    pd = p.shape[2]
    tm = min(OUT_ROW_TILE, t)
    assert t % tm == 0

    def const(shape):
        return pl.BlockSpec(shape, lambda i: (0, 0))

    return pl.pallas_call(
        functools.partial(_out_ple_body, final=final),
        grid=(t // tm,),
        in_specs=[
            pl.BlockSpec((tm, mix_w), lambda i: (i, 0)),
            pl.BlockSpec((tm, d), lambda i: (i, 0)),
            pl.BlockSpec((None, tm, pd), lambda i: (layer, i, 0)),
            const((mix_w, d)),
            const((1, d)),
            const((d, d)),
            const((pd, d)),
            const((1, d)),
        ],
        out_specs=pl.BlockSpec((tm, d), lambda i: (i, 0)),
        out_shape=jax.ShapeDtypeStruct((t, d), F32),
        compiler_params=pltpu.CompilerParams(
            dimension_semantics=("parallel",), vmem_limit_bytes=VMEM_LIMIT),
        name="out_ple",
    )(o, h, p, w_out, ple_norm.reshape(1, d), w_gate, w_proj, final_norm.reshape(1, d))


def kernel(x, p, norm_mix, gla_w_in, gla_w_gk2, gla_b_gk, gla_norm, gla_w_out, kv_norm, w_kv, diff_w_in,
           diff_lambda, diff_norm, diff_w_out, ple_norm, ple_w_gate, ple_w_proj, final_norm):
    batch, seq, d = x.shape
    depth = p.shape[0]
    n_gla = gla_w_in.shape[0]
    t = batch * seq
    key_w = gla_w_gk2.shape[2]
    main_cols = gla_w_in.shape[2] - GLA_GATE_RANK

    h = x.reshape(t, d)
    p = p.reshape(depth, t, p.shape[3])
    kv = None
    for i in range(depth):
        if i < n_gla:
            w_main = gla_w_in[i, :, :main_cols].astype(BF16)
            w_gate = gla_w_in[i, :, main_cols:]
            w_lr = jnp.zeros((d, LANES), F32).at[:, :3 * GLA_GATE_RANK].set(jnp.concatenate([w_gate] * 3, axis=1))
            w_lr = jnp.concatenate(_split_bf16(w_lr), axis=1)
            proj, glr = _norm_matmul(h, norm_mix[i], w_main, scaled_cols=key_w,
                                     scale=(key_w // GLA_HEADS) ** -0.5, w_lowrank=w_lr)
            o = _gla_mixer(proj, glr, gla_w_gk2[i], gla_b_gk[i], gla_norm[i], batch=batch, seq=seq)
            w_out = gla_w_out[i]
        else:
            j = i - n_gla
            lambda_init = 0.8 - 0.6 * math.exp(-0.3 * i)
            qg = _norm_matmul(h, norm_mix[i], diff_w_in[j].astype(BF16), scaled_cols=diff_w_in.shape[2] // 2,
                              scale=DIFF_HEAD_DIM ** -0.5 * LOG2E)
            o = _diff_mixer(qg, kv, diff_lambda[j], diff_norm[j], lambda_init, batch=batch, seq=seq)
            w_out = diff_w_out[j]
        h = _out_ple(o, h, p, i, w_out.astype(BF16), ple_norm[i], ple_w_gate[i].astype(BF16),
                     ple_w_proj[i].astype(BF16), final_norm, final=(i == depth - 1))
        if i == n_gla - 1:
            kv = _norm_matmul(h, kv_norm, w_kv.astype(BF16))
    return h.reshape(batch, seq, d)
```

```python
import functools
import math

import jax
import jax.numpy as jnp
import numpy as np
from jax import lax
from jax.experimental import pallas as pl
from jax.experimental.pallas import tpu as pltpu

F32 = jnp.float32
BF16 = jnp.bfloat16
EPS = 1e-6
LOG2E = math.log2(math.e)

EXPAND = 2
PLE_DIM = 256
GLA_HEADS = 4
GLA_GATE_RANK = 16
GLA_GATE_NORMALIZER = 16.0
DIFF_HEAD_DIM = 128

LANES = 128
VMEM_LIMIT = 56 * 1024 * 1024

GLA_CHUNK = 128
GLA_ROWS = 4096
GLA_STAGE_ORDER = (0, 3, 1, 2)
GLA_ROUNDS_PER_TRIP = 4
ATTN_TILE = 512
ATTN_STAGE_ORDER = (2, 1, 0)
ATTN_ROUNDS_PER_TRIP = 4
ROW_TILE = 1024
OUT_ROW_TILE = 1024
COL_TILE = 1024


def _rms(x, g):
    return x * lax.rsqrt(jnp.mean(x * x, axis=-1, keepdims=True) + EPS) * g


def _sigmoid(x):
    return 0.5 * jnp.tanh(0.5 * x) + 0.5


def _silu(x):
    h = 0.5 * x
    return h + h * jnp.tanh(h)


def _split_bf16(x):
    hi = x.astype(BF16)
    lo = (x - hi.astype(F32)).astype(BF16)
    return hi, lo


def _dot(a, b):
    return jnp.dot(a, b, preferred_element_type=F32)


def _dot_nt(a, b):
    return lax.dot_general(a, b, (((1,), (1,)), ((), ())), preferred_element_type=F32)


def _dot_tn(a, b):
    return lax.dot_general(a, b, (((0,), (0,)), ((), ())), preferred_element_type=F32)


def _dot_split(a, b):
    ah, al = _split_bf16(a)
    bh, bl = _split_bf16(b)
    return _dot(ah, bh) + _dot(al, bh) + _dot(ah, bl)


def _tile_lanes(x, n):
    return jnp.concatenate([x] * n, axis=1)


def _rows(i, size):
    if isinstance(i, int):
        return pl.ds(i * size, size)
    return pl.ds(pl.multiple_of(i * size, size), size)


def _software_pipeline(n_items, stages, order, rounds_per_trip, static_items=0):
    depth = len(stages)
    n_rounds = n_items + depth - 1
    assert sorted(order) == list(range(depth)) and rounds_per_trip % 2 == 0

    def static_round(r):
        for k in order:
            if 0 <= r - k < n_items:
                stages[k](r - k, (r - k) % 2)

    n_steady = max(n_items - static_items - depth + 1, 0)
    n_trips = n_steady // rounds_per_trip
    first = n_items - n_trips * rounds_per_trip if n_trips else n_rounds
    for r in range(first):
        static_round(r)
    if n_trips:
        def trip(i, carry):
            for dr in range(rounds_per_trip):
                for k in order:
                    stages[k](first + rounds_per_trip * i + dr - k, (first + dr - k) % 2)
            return carry

        lax.fori_loop(0, n_trips, trip, 0)
    for r in range(n_items if n_trips else n_rounds, n_rounds):
        static_round(r)


def _norm_matmul_body(*refs, col_tile, scaled_cols, scale, lowrank):
    if lowrank:
        x_ref, g_ref, w_ref, wl_ref, o_ref, lr_ref = refs
    else:
        x_ref, g_ref, w_ref, o_ref = refs
    tm = x_ref.shape[0]
    for r in (slice(0, tm // 2), slice(tm // 2, tm)):
        xn = _rms(x_ref[r, :], g_ref[...])
        xh = xn.astype(BF16)
        if lowrank:
            xl = (xn - xh.astype(F32)).astype(BF16)
            both = _dot(xh, wl_ref[...])
            lr_ref[r, :] = both[:, :LANES] + both[:, LANES:] + _dot(xl, wl_ref[:, :LANES])
        for c0 in range(0, o_ref.shape[1], col_tile):
            acc = _dot(xh, w_ref[:, c0:c0 + col_tile])
            if c0 < scaled_cols:
                acc = acc * scale
            o_ref[r, c0:c0 + col_tile] = acc.astype(o_ref.dtype)


def _norm_matmul(x, g, w, *, scaled_cols=0, scale=1.0, w_lowrank=None):
    t, d = x.shape
    n = w.shape[1]
    tm = min(ROW_TILE, t)
    tn = min(COL_TILE, n)
    assert t % tm == 0 and n % tn == 0 and scaled_cols % tn == 0
    lowrank = w_lowrank is not None

    def resident(shape):
        return pl.BlockSpec(shape, lambda i: (0, 0), pipeline_mode=pl.Buffered(1))

    in_specs = [pl.BlockSpec((tm, d), lambda i: (i, 0)), resident((1, d)), resident((d, n))]
    args = [x, g.reshape(1, d), w]
    out_shape = [jax.ShapeDtypeStruct((t, n), BF16)]
    out_specs = [pl.BlockSpec((tm, n), lambda i: (i, 0))]
    if lowrank:
        in_specs.append(resident((d, 2 * LANES)))
        args.append(w_lowrank)
        out_shape.append(jax.ShapeDtypeStruct((t, LANES), F32))
        out_specs.append(pl.BlockSpec((tm, LANES), lambda i: (i, 0)))
    out = pl.pallas_call(
        functools.partial(_norm_matmul_body, col_tile=tn, scaled_cols=scaled_cols, scale=scale, lowrank=lowrank),
        grid=(t // tm,),
        in_specs=in_specs,
        out_specs=out_specs,
        out_shape=out_shape,
        compiler_params=pltpu.CompilerParams(dimension_semantics=("parallel",), vmem_limit_bytes=VMEM_LIMIT),
        name="norm_matmul",
    )(*args)
    return out if lowrank else out[0]


def _log_sigmoid(x):
    return jnp.minimum(x, 0.0) - jnp.log(1.0 + jnp.exp(-jnp.abs(x)))


def _gla_tables(c):
    n_lev = c.bit_length() - 1
    assert 1 << n_lev == c
    i = np.arange(c)[:, None]
    t = np.arange(c)[None, :]
    sums = [t <= i, t > i]
    pairs = []
    for lev in range(n_lev):
        s = c >> (lev + 1)
        blk = i // s
        odd = blk % 2 == 1
        sums.append(np.where(odd, (t >= blk * s) & (t <= i), (t > i) & (t < (blk + 1) * s)))
        pairs.append(odd & (t // s == blk - 1))
    pairs.append(i == t)
    pairs = np.stack(pairs).astype(np.float32)
    assert (pairs.sum(0) == np.tril(np.ones((c, c)))).all()
    return np.concatenate(sums).astype(np.float32), pairs


def _gla_body(q_ref, k_ref, v_ref, gate_ref, glr_ref, w2_ref, bias_ref, gn_ref, sums_ref, pairs_ref, o_ref,
              st_ref, stb_ref, *stage_refs, chunk):
    la_refs, e_refs, ql_refs, kl_refs, ab_refs, qb_refs, kd_refs, dec_refs = (
        stage_refs[2 * i:2 * i + 2] for i in range(8))
    c = chunk
    n_lev = pairs_ref.shape[0] - 1
    n_chunks = q_ref.shape[0] // c
    rank = GLA_GATE_RANK

    @pl.when(pl.program_id(2) == 0)
    def _():
        st_ref[...] = jnp.zeros_like(st_ref)
        stb_ref[...] = jnp.zeros_like(stb_ref)

    lane = lax.broadcasted_iota(jnp.int32, (c, LANES), 1)
    lo_lanes = lane // rank == 1

    def gates(i, slot):
        glr = glr_ref[_rows(i, c), :]
        hi = glr.astype(BF16)
        lo = (glr - hi.astype(F32)).astype(BF16)
        gk = _dot(jnp.where(lo_lanes, lo, hi), w2_ref[...]) + bias_ref[...]
        la = _log_sigmoid(gk) * (LOG2E / GLA_GATE_NORMALIZER)
        la_hi, la_lo = _split_bf16(la)
        la_refs[slot][0:c, :] = la_hi
        la_refs[slot][c:2 * c, :] = la_lo

    def decays(i, slot):
        rows = _rows(i, c)
        e = _dot(sums_ref[...], la_refs[slot][...])
        e_refs[slot][...] = e[0:2 * c]
        qf = q_ref[rows, :].astype(F32)
        kf = k_ref[rows, :].astype(F32)
        for lev in range(n_lev):
            w = jnp.exp2(e[(lev + 2) * c:(lev + 3) * c])
            ql_refs[slot][lev] = (qf * w).astype(BF16)
            kl_refs[slot][lev] = (kf * w).astype(BF16)

    def scores(i, slot):
        rows = _rows(i, c)
        q = q_ref[rows, :]
        k = k_ref[rows, :]
        a = pairs_ref[n_lev] * _dot_nt(q, k)
        for lev in range(n_lev):
            a = a + pairs_ref[lev] * _dot_nt(ql_refs[slot][lev], kl_refs[slot][lev])
        ab_refs[slot][...] = a.astype(BF16)
        e = e_refs[slot][...]
        b = e[0:c]
        qb_refs[slot][...] = (q.astype(F32) * jnp.exp2(b)).astype(BF16)
        kd_refs[slot][...] = (k.astype(F32) * jnp.exp2(e[c:2 * c])).astype(BF16)
        dec_refs[slot][...] = jnp.broadcast_to(jnp.exp2(b[c - 1:c, :]), dec_refs[slot].shape)

    def output(i, slot):
        rows = _rows(i, c)
        v = v_ref[rows, :]
        o = _dot_nt(qb_refs[slot][...], stb_ref[...]) + _dot(ab_refs[slot][...], v)
        st = st_ref[...] * dec_refs[slot][0:1, :] + _dot_tn(v, kd_refs[slot][...])
        st_ref[...] = st
        stb_ref[...] = st.astype(BF16)
        g = gate_ref[rows, :].astype(F32)
        o_ref[rows, :] = (_rms(o, gn_ref[...]) * _silu(g)).astype(o_ref.dtype)

    _software_pipeline(n_chunks, (gates, decays, scores, output), GLA_STAGE_ORDER, GLA_ROUNDS_PER_TRIP)


def _gla_mixer(proj, glr, w_gk2, b_gk, g_norm, *, batch, seq):
    t = proj.shape[0]
    key_w = w_gk2.shape[1]
    dk = key_w // GLA_HEADS
    mix_w = 2 * key_w
    dv = mix_w // GLA_HEADS
    rows = min(GLA_ROWS, seq)
    nblk = seq // rows
    c = min(GLA_CHUNK, rows)
    assert seq % rows == 0 and rows % c == 0
    k_off = key_w // dk
    v_off = 2 * key_w // dv
    g_off = (2 * key_w + mix_w) // dv
    w_hi, w_lo = _split_bf16(w_gk2)
    w2 = jnp.zeros((LANES, key_w), BF16).at[:3 * GLA_GATE_RANK].set(jnp.concatenate([w_hi, w_hi, w_lo]))
    sums, pairs = _gla_tables(c)
    n_lev = pairs.shape[0] - 1
    sums = jnp.asarray(np.concatenate([sums, sums], axis=1), BF16)
    pairs = jnp.asarray(pairs, F32)

    def rowmap(off):
        return lambda b, h, n: (b * nblk + n, off + h)

    def two(shape, dtype):
        return [pltpu.VMEM(shape, dtype), pltpu.VMEM(shape, dtype)]

    return pl.pallas_call(
        functools.partial(_gla_body, chunk=c),
        grid=(batch, GLA_HEADS, nblk),
        in_specs=[
            pl.BlockSpec((rows, dk), rowmap(0)),
            pl.BlockSpec((rows, dk), rowmap(k_off)),
            pl.BlockSpec((rows, dv), rowmap(v_off)),
            pl.BlockSpec((rows, dv), rowmap(g_off)),
            pl.BlockSpec((rows, LANES), lambda b, h, n: (b * nblk + n, 0)),
            pl.BlockSpec((LANES, dk), lambda b, h, n: (0, h)),
            pl.BlockSpec((1, dk), lambda b, h, n: (0, h)),
            pl.BlockSpec((1, dv), lambda b, h, n: (0, 0)),
            pl.BlockSpec(sums.shape, lambda b, h, n: (0, 0)),
            pl.BlockSpec(pairs.shape, lambda b, h, n: (0, 0, 0)),
        ],
        out_specs=pl.BlockSpec((rows, dv), rowmap(0)),
        out_shape=jax.ShapeDtypeStruct((t, mix_w), BF16),
        scratch_shapes=(
            [pltpu.VMEM((dv, dk), F32), pltpu.VMEM((dv, dk), BF16)]
            + two((2 * c, dk), BF16)
            + two((2 * c, dk), F32)
            + two((n_lev, c, dk), BF16) + two((n_lev, c, dk), BF16)
            + two((c, c), BF16)
            + two((c, dk), BF16) + two((c, dk), BF16)
            + two((8, dk), F32)),
        compiler_params=pltpu.CompilerParams(
            dimension_semantics=("parallel", "parallel", "arbitrary"), vmem_limit_bytes=VMEM_LIMIT),
        name="gla_mixer",
    )(proj, proj, proj, proj, glr, w2, b_gk.reshape(1, key_w), g_norm.reshape(1, dv), sums, pairs)


def _diff_attn_body(qt_ref, kt_ref, q_ref, k_ref, v_ref, gate_ref, lam_ref, gn_ref, o_ref, s0_ref, s1_ref,
                    mx0_ref, mx1_ref, p0_ref, p1_ref, al0_ref, al1_ref, m_ref, l_ref, acc_ref,
                    *, tile, tiles, lambda_init):
    d = DIFF_HEAD_DIM
    n_tiles = q_ref.shape[0] // tile
    s_refs, mx_refs = (s0_ref, s1_ref), (mx0_ref, mx1_ref)
    p_refs, al_refs = (p0_ref, p1_ref), (al0_ref, al1_ref)
    reps = tile // LANES
    wide = 2 * d // LANES

    def item(n):
        return tiles[n] if isinstance(n, int) else (qt_ref[n], kt_ref[n])

    def is_diagonal(n):
        return isinstance(n, int) and n < n_tiles

    half = tile // 2
    bands = ((0, half, half), (half, tile, tile))

    def scores(n, slot):
        qt, kt = item(n)
        for t in range(2):
            cols = slice(t * d, (t + 1) * d)
            if is_diagonal(n):
                for r0, r1, nk in bands:
                    s_refs[slot][t, r0:r1, 0:nk] = _dot_nt(q_ref[pl.ds(qt * tile + r0, r1 - r0), cols],
                                                           k_ref[pl.ds(kt * tile, nk), cols])
            else:
                s = _dot_nt(q_ref[_rows(qt, tile), cols], k_ref[_rows(kt, tile), cols])
                s_refs[slot][t] = s
                mx_refs[slot][t] = jnp.broadcast_to(jnp.max(s, axis=1, keepdims=True), (tile, LANES))

    def weights(n, slot):
        qt = item(n)[0]
        for t in range(2):
            if is_diagonal(n):
                for r0, r1, nk in bands:
                    q_rows = pl.ds(qt * tile + r0, r1 - r0)
                    s = s_refs[slot][t, r0:r1, 0:nk]
                    row = lax.broadcasted_iota(jnp.int32, s.shape, 0) + r0
                    col = lax.broadcasted_iota(jnp.int32, s.shape, 1)
                    s = jnp.where(col <= row, s, -1e30)
                    m_new = jnp.broadcast_to(jnp.max(s, axis=1, keepdims=True), (r1 - r0, LANES))
                    p = jnp.exp2(s - _tile_lanes(m_new, nk // LANES))
                    l_ref[t, q_rows, :] = jnp.broadcast_to(jnp.sum(p, axis=1, keepdims=True), (r1 - r0, LANES))
                    m_ref[t, q_rows, :] = m_new
                    p_refs[slot][t, r0:r1, 0:nk] = p.astype(BF16)
            else:
                q_rows = _rows(qt, tile)
                m_old = m_ref[t, q_rows, :]
                m_new = jnp.maximum(m_old, mx_refs[slot][t])
                alpha = jnp.exp2(m_old - m_new)
                p = jnp.exp2(s_refs[slot][t] - _tile_lanes(m_new, reps))
                l_ref[t, q_rows, :] = alpha * l_ref[t, q_rows, :] + jnp.sum(p, axis=1, keepdims=True)
                m_ref[t, q_rows, :] = m_new
                al_refs[slot][t] = alpha
                p_refs[slot][t] = p.astype(BF16)

    def values(n, slot):
        qt, kt = item(n)
        for t in range(2):
            if is_diagonal(n):
                for r0, r1, nk in bands:
                    acc_ref[t, pl.ds(qt * tile + r0, r1 - r0), :] = _dot(p_refs[slot][t, r0:r1, 0:nk],
                                                                        v_ref[pl.ds(kt * tile, nk), :])
            else:
                q_rows = _rows(qt, tile)
                acc_ref[t, q_rows, :] = (acc_ref[t, q_rows, :] * _tile_lanes(al_refs[slot][t], wide)
                                         + _dot(p_refs[slot][t], v_ref[_rows(kt, tile), :]))

    _software_pipeline(len(tiles), (scores, weights, values), ATTN_STAGE_ORDER, ATTN_ROUNDS_PER_TRIP,
                       static_items=n_tiles)

    lam = lam_ref[...]
    lam_full = (jnp.exp(jnp.sum(lam[0:1] * lam[1:2], axis=1, keepdims=True))
                - jnp.exp(jnp.sum(lam[2:3] * lam[3:4], axis=1, keepdims=True)) + lambda_init)

    gain = gn_ref[...] * (1.0 - lambda_init)

    def finish(qt, carry):
        rows = _rows(qt, tile)
        o1 = acc_ref[0, rows, :] * _tile_lanes(1.0 / l_ref[0, rows, :], wide)
        o2 = acc_ref[1, rows, :] * _tile_lanes(lam_full / l_ref[1, rows, :], wide)
        o = _rms(o1 - o2, gain)
        g = gate_ref[rows, :].astype(F32)
        o_ref[rows, :] = (o * _silu(g)).astype(o_ref.dtype)
        return carry

    lax.fori_loop(0, n_tiles, finish, 0)


def _diff_mixer(qg, kv, lam, g_norm, lambda_init, *, batch, seq):
    t = qg.shape[0]
    mix_w = qg.shape[1] // 2
    hw = 2 * DIFF_HEAD_DIM
    heads = mix_w // hw
    tile = min(ATTN_TILE, seq)
    nq = seq // tile
    assert seq % tile == 0
    tiles = tuple((i, i) for i in range(nq)) + tuple((i, j) for i in range(nq) for j in range(i))
    qt_tab = jnp.asarray([qt for qt, _ in tiles], jnp.int32)
    kt_tab = jnp.asarray([kt for _, kt in tiles], jnp.int32)

    def two(shape, dtype):
        return [pltpu.VMEM(shape, dtype), pltpu.VMEM(shape, dtype)]

    return pl.pallas_call(
        functools.partial(_diff_attn_body, tile=tile, tiles=tiles, lambda_init=lambda_init),
        grid_spec=pltpu.PrefetchScalarGridSpec(
            num_scalar_prefetch=2,
            grid=(batch, heads),
            in_specs=[
                pl.BlockSpec((seq, hw), lambda b, h, *_: (b, h)),
                pl.BlockSpec((seq, hw), lambda b, h, *_: (b, h)),
                pl.BlockSpec((seq, hw), lambda b, h, *_: (b, heads + h)),
                pl.BlockSpec((seq, hw), lambda b, h, *_: (b, heads + h)),
                pl.BlockSpec((4, DIFF_HEAD_DIM), lambda b, h, *_: (0, 0)),
                pl.BlockSpec((1, hw), lambda b, h, *_: (0, 0)),
            ],
            out_specs=pl.BlockSpec((seq, hw), lambda b, h, *_: (b, h)),
            scratch_shapes=(
                two((2, tile, tile), F32)
                + two((2, tile, LANES), F32)
                + two((2, tile, tile), BF16)
                + two((2, tile, LANES), F32)
                + [pltpu.VMEM((2, seq, LANES), F32),
                   pltpu.VMEM((2, seq, LANES), F32),
                   pltpu.VMEM((2, seq, hw), F32)]),
        ),
        out_shape=jax.ShapeDtypeStruct((t, mix_w), BF16),
        compiler_params=pltpu.CompilerParams(
            dimension_semantics=("parallel", "parallel"), vmem_limit_bytes=VMEM_LIMIT),
        name="diff_attn",
    )(qt_tab, kt_tab, qg, kv, kv, qg, lam, g_norm.reshape(1, hw))


def _out_ple_body(o_ref, h_ref, p_ref, wo_ref, pn_ref, wg_ref, wp_ref, fn_ref, out_ref, *, final):
    tm = h_ref.shape[0]
    halves = (slice(0, tm // 2), slice(tm // 2, tm))
    hm = [h_ref[r, :] + _dot(o_ref[r, :], wo_ref[...]) for r in halves]
    emb = [_dot(p_ref[r, :].astype(BF16), wp_ref[...]) for r in halves]
    for r, hm_r, emb_r in zip(halves, hm, emb):
        hn = _rms(hm_r, pn_ref[...]).astype(BF16)
        h2 = hm_r + _sigmoid(_dot(hn, wg_ref[...])) * emb_r
        if final:
            h2 = _rms(h2, fn_ref[...])
        out_ref[r, :] = h2


def _out_ple(o, h, p, layer, w_out, ple_norm, w_gate, w_proj, final_norm, *, final):
    t, d = h.shape
    mix_w = o.shape[1]
    pd = p.shape[2]
    tm = min(OUT_ROW_TILE, t)
    assert t % tm == 0

    def const(shape):
        return pl.BlockSpec(shape, lambda i: (0, 0))

    return pl.pallas_call(
        functools.partial(_out_ple_body, final=final),
        grid=(t // tm,),
        in_specs=[
            pl.BlockSpec((tm, mix_w), lambda i: (i, 0)),
            pl.BlockSpec((tm, d), lambda i: (i, 0)),
            pl.BlockSpec((None, tm, pd), lambda i: (layer, i, 0)),
            const((mix_w, d)),
            const((1, d)),
            const((d, d)),
            const((pd, d)),
            const((1, d)),
        ],
        out_specs=pl.BlockSpec((tm, d), lambda i: (i, 0)),
        out_shape=jax.ShapeDtypeStruct((t, d), F32),
        compiler_params=pltpu.CompilerParams(
            dimension_semantics=("parallel",), vmem_limit_bytes=VMEM_LIMIT),
        name="out_ple",
    )(o, h, p, w_out, ple_norm.reshape(1, d), w_gate, w_proj, final_norm.reshape(1, d))


def kernel(x, p, norm_mix, gla_w_in, gla_w_gk2, gla_b_gk, gla_norm, gla_w_out, kv_norm, w_kv, diff_w_in,
           diff_lambda, diff_norm, diff_w_out, ple_norm, ple_w_gate, ple_w_proj, final_norm):
    batch, seq, d = x.shape
    depth = p.shape[0]
    n_gla = gla_w_in.shape[0]
    t = batch * seq
    key_w = gla_w_gk2.shape[2]
    main_cols = gla_w_in.shape[2] - GLA_GATE_RANK

    h = x.reshape(t, d)
    p = p.reshape(depth, t, p.shape[3])
    kv = None
    for i in range(depth):
        if i < n_gla:
            w_main = gla_w_in[i, :, :main_cols].astype(BF16)
            w_gate = gla_w_in[i, :, main_cols:]
            w_lr = jnp.zeros((d, LANES), F32).at[:, :3 * GLA_GATE_RANK].set(jnp.concatenate([w_gate] * 3, axis=1))
            w_lr = jnp.concatenate(_split_bf16(w_lr), axis=1)
            proj, glr = _norm_matmul(h, norm_mix[i], w_main, scaled_cols=key_w,
                                     scale=(key_w // GLA_HEADS) ** -0.5, w_lowrank=w_lr)
            o = _gla_mixer(proj, glr, gla_w_gk2[i], gla_b_gk[i], gla_norm[i], batch=batch, seq=seq)
            w_out = gla_w_out[i]
        else:
            j = i - n_gla
            lambda_init = 0.8 - 0.6 * math.exp(-0.3 * i)
            qg = _norm_matmul(h, norm_mix[i], diff_w_in[j].astype(BF16), scaled_cols=diff_w_in.shape[2] // 2,
                              scale=DIFF_HEAD_DIM ** -0.5 * LOG2E)
            o = _diff_mixer(qg, kv, diff_lambda[j], diff_norm[j], lambda_init, batch=batch, seq=seq)
            w_out = diff_w_out[j]
        h = _out_ple(o, h, p, i, w_out.astype(BF16), ple_norm[i], ple_w_gate[i].astype(BF16),
                     ple_w_proj[i].astype(BF16), final_norm, final=(i == depth - 1))
        if i == n_gla - 1:
            kv = _norm_matmul(h, kv_norm, w_kv.astype(BF16))
    return h.reshape(batch, seq, d)
```
